```python
import jax, jax.numpy as jnp
from jax import lax
import numpy as np

D_MODEL = 1024
BATCH = 1
SEQ = 16384
DEPTH = 2

ROPE_THETA = 500000.0
EPS = 1e-6
NEG = -1e30
Q_BLOCK = 128

GROUP_WIDTH = D_MODEL // 4
D_MIX = 4 * GROUP_WIDTH

MLA_HEADS = 4
MLA_NOPE = 64
MLA_ROPE = 32
MLA_QK = MLA_NOPE + MLA_ROPE
MLA_V = GROUP_WIDTH // MLA_HEADS
MLA_Q_RANK = 256
MLA_KV_RANK = 128

CONV_WIDTH = 3

SG_CHUNK = 128
SG_GROUPS = 4
SG_GDIM = GROUP_WIDTH // SG_GROUPS

MOBA_HEADS = 4
MOBA_HD = GROUP_WIDTH // MOBA_HEADS
MOBA_ROT = MOBA_HD // 4
MOBA_BLOCK = 256
MOBA_TOPK = 3

SPLIT_SIZES = (
    MLA_Q_RANK, MLA_KV_RANK, MLA_ROPE, GROUP_WIDTH,
    GROUP_WIDTH, GROUP_WIDTH, GROUP_WIDTH, GROUP_WIDTH,
    GROUP_WIDTH, GROUP_WIDTH, GROUP_WIDTH,
    GROUP_WIDTH, GROUP_WIDTH, GROUP_WIDTH, GROUP_WIDTH,
)
D_IN = 3488

kernel_name = "hybrid_parallel_mla_conv_sgmlp_moba"


def rmsnorm(x, g):
    xf = x.astype(jnp.float32)
    y = xf * lax.rsqrt(jnp.mean(xf * xf, axis=-1, keepdims=True) + EPS)
    return (y * g).astype(x.dtype)


def layernorm(x, g, b):
    xf = x.astype(jnp.float32)
    mu = jnp.mean(xf, axis=-1, keepdims=True)
    var = jnp.mean(jnp.square(xf - mu), axis=-1, keepdims=True)
    y = (xf - mu) * lax.rsqrt(var + EPS)
    return (y * g + b).astype(x.dtype)


def rope(x, pos):
    rd = x.shape[-1]
    half = rd // 2
    inv = jnp.power(ROPE_THETA, -jnp.arange(half, dtype=jnp.float32) * 2.0 / rd)
    ang = pos.astype(jnp.float32)[..., None] * inv
    cos = jnp.cos(ang)[:, :, None, :]
    sin = jnp.sin(ang)[:, :, None, :]
    xf = x.astype(jnp.float32)
    x1, x2 = xf[..., :half], xf[..., half:]
    return jnp.concatenate([x1 * cos - x2 * sin, x2 * cos + x1 * sin], axis=-1).astype(x.dtype)


def causal_attention_blocks(q, k, v, scale):
    B, S, H, D = q.shape
    nq = S // Q_BLOCK
    qb = q.reshape(B, nq, Q_BLOCK, H, D).swapaxes(0, 1)
    kpos = jnp.arange(S)

    def one(args):
        qblk, i = args
        s = jnp.einsum('bqhd,bkhd->bhqk', qblk, k, preferred_element_type=jnp.float32) * scale
        qpos = i * Q_BLOCK + jnp.arange(Q_BLOCK)
        s = jnp.where(kpos[None, :] <= qpos[:, None], s, NEG)
        p = jax.nn.softmax(s, axis=-1)
        return jnp.einsum('bhqk,bkhd->bqhd', p.astype(v.dtype), v)

    out = lax.map(one, (qb, jnp.arange(nq)))
    return out.swapaxes(0, 1).reshape(B, S, H, v.shape[-1])


def moba_attention(q, k, v, scale):
    B, S, H, D = q.shape
    nb = -(-S // MOBA_BLOCK)
    Sp = nb * MOBA_BLOCK
    kk = min(MOBA_TOPK, nb)
    pad = ((0, 0), (0, Sp - S), (0, 0), (0, 0))
    kb = jnp.pad(k, pad).reshape(B, nb, MOBA_BLOCK, H, D).transpose(0, 3, 1, 2, 4)
    vb = jnp.pad(v, pad).reshape(B, nb, MOBA_BLOCK, H, D).transpose(0, 3, 1, 2, 4)
    kmean = jnp.mean(kb.astype(jnp.float32), axis=3)
    gate = jnp.einsum('bshd,bhnd->bhsn', q.astype(jnp.float32), kmean)
    qblk = jnp.arange(S) // MOBA_BLOCK
    past = jnp.arange(nb)[None, :] < qblk[:, None]
    gate = jnp.where(past, gate, NEG)
    _, idx = lax.top_k(gate, kk)
    valid = jnp.arange(kk)[None, :] < qblk[:, None]

    nq = S // Q_BLOCK
    qc = q.reshape(B, nq, Q_BLOCK, H, D).transpose(1, 0, 3, 2, 4)
    idxc = idx.reshape(B, H, nq, Q_BLOCK, kk).transpose(2, 0, 1, 3, 4)
    validc = valid.reshape(nq, Q_BLOCK, kk)
    bi = jnp.arange(B)[:, None, None, None]
    hi = jnp.arange(H)[None, :, None, None]
    n_sel = kk * MOBA_BLOCK

    def one(args):
        qx, ix, vx, i = args
        kg = kb[bi, hi, ix]
        vg = vb[bi, hi, ix]
        s_sel = jnp.einsum('bhqd,bhqrkd->bhqrk', qx, kg, preferred_element_type=jnp.float32) * scale
        s_sel = jnp.where(vx[None, None, :, :, None], s_sel, NEG).reshape(B, H, Q_BLOCK, n_sel)
        qpos = i * Q_BLOCK + jnp.arange(Q_BLOCK)
        own = (i * Q_BLOCK) // MOBA_BLOCK
        ko = lax.dynamic_index_in_dim(kb, own, axis=2, keepdims=False)
        vo = lax.dynamic_index_in_dim(vb, own, axis=2, keepdims=False)
        s_own = jnp.einsum('bhqd,bhkd->bhqk', qx, ko, preferred_element_type=jnp.float32) * scale
        kpos = own * MOBA_BLOCK + jnp.arange(MOBA_BLOCK)
        s_own = jnp.where(kpos[None, :] <= qpos[:, None], s_own, NEG)
        p = jax.nn.softmax(jnp.concatenate([s_sel, s_own], axis=-1), axis=-1).astype(v.dtype)
        p_sel = p[..., :n_sel].reshape(B, H, Q_BLOCK, kk, MOBA_BLOCK)
        p_own = p[..., n_sel:]
        return (jnp.einsum('bhqrk,bhqrkd->bhqd', p_sel, vg)
                + jnp.einsum('bhqk,bhkd->bhqd', p_own, vo))

    out = lax.map(one, (qc, idxc, validc, jnp.arange(nq)))
    return out.transpose(1, 0, 3, 2, 4).reshape(B, S, H, D)


def short_conv(z, w):
    C = z.shape[-1]
    return lax.conv_general_dilated(
        z, w[:, None, :].astype(z.dtype), window_strides=(1,), padding=[(CONV_WIDTH - 1, 0)],
        dimension_numbers=('NWC', 'WIO', 'NWC'), feature_group_count=C)


def spatial_gate(u, v, ln_g, ln_b, w_s, b_s):
    B, S, C = v.shape
    nc = S // SG_CHUNK
    vn = layernorm(v, ln_g, ln_b).reshape(B, nc, SG_CHUNK, SG_GROUPS, SG_GDIM)
    mask = jnp.tril(jnp.ones((SG_CHUNK, SG_CHUNK), dtype=bool))
    ws = jnp.where(mask[None], w_s, jnp.zeros_like(w_s))
    mixed = jnp.einsum('gts,bcsgd->bctgd', ws, vn) + b_s.T[None, None, :, :, None]
    return u * mixed.reshape(B, S, C)


def hybrid_layer(x, positions, norm_g, w_in, mla_q_norm_g, mla_w_uq, mla_kv_norm_g, mla_w_ukv,
                 mla_q_g, mla_k_nope_g, mla_k_rope_g, conv_w, sg_ln_g, sg_ln_b, sg_w, sg_b,
                 moba_q_g, moba_k_g, w_out):
    B, S, _ = x.shape
    h = rmsnorm(x, norm_g)
    p = h @ w_in
    offs = np.cumsum(np.array(SPLIT_SIZES))[:-1].tolist()
    (cq, ckv, kr, g_a, c_h, c_b, c_c, g_b, sg_u, sg_v, g_c, mq, mk, mv, g_d) = jnp.split(p, offs, axis=-1)

    q = (rmsnorm(cq, mla_q_norm_g) @ mla_w_uq).reshape(B, S, MLA_HEADS, MLA_QK)
    kv = (rmsnorm(ckv, mla_kv_norm_g) @ mla_w_ukv).reshape(B, S, MLA_HEADS, MLA_NOPE + MLA_V)
    k_nope, v_a = kv[..., :MLA_NOPE], kv[..., MLA_NOPE:]
    q = rmsnorm(q, mla_q_g)
    q = jnp.concatenate([q[..., :MLA_NOPE], rope(q[..., MLA_NOPE:], positions)], axis=-1)
    k_nope = rmsnorm(k_nope, mla_k_nope_g)
    k_rope = rope(rmsnorm(kr, mla_k_rope_g)[:, :, None, :], positions)
    k = jnp.concatenate([k_nope, jnp.broadcast_to(k_rope, (B, S, MLA_HEADS, MLA_ROPE))], axis=-1)
    o_a = causal_attention_blocks(q, k, v_a, MLA_QK ** -0.5).reshape(B, S, GROUP_WIDTH)

    o_b = c_b * short_conv(c_c * c_h, conv_w)

    o_c = spatial_gate(sg_u, sg_v, sg_ln_g, sg_ln_b, sg_w, sg_b)

    qd = rmsnorm(mq.reshape(B, S, MOBA_HEADS, MOBA_HD), moba_q_g)
    kd = rmsnorm(mk.reshape(B, S, MOBA_HEADS, MOBA_HD), moba_k_g)
    vd = mv.reshape(B, S, MOBA_HEADS, MOBA_HD)
    qd = jnp.concatenate([rope(qd[..., :MOBA_ROT], positions), qd[..., MOBA_ROT:]], axis=-1)
    kd = jnp.concatenate([rope(kd[..., :MOBA_ROT], positions), kd[..., MOBA_ROT:]], axis=-1)
    o_d = moba_attention(qd, kd, vd, MOBA_HD ** -0.5).reshape(B, S, GROUP_WIDTH)

    y = jnp.concatenate([o_a * jax.nn.silu(g_a), o_b * jax.nn.silu(g_b),
                         o_c * jax.nn.silu(g_c), o_d * jax.nn.silu(g_d)], axis=-1)
    return y @ w_out


def setup_inputs(seed: int = 0) -> dict:
    key = jax.random.key(seed)
    ks = jax.random.split(key, 24)
    L, D, GW = DEPTH, D_MODEL, GROUP_WIDTH
    f32 = jnp.float32

    def nrm(k, shape, scale):
        return jax.random.normal(k, shape, f32) * scale

    def gain(k, shape):
        return 1.0 + 0.05 * jax.random.normal(k, shape, f32)

    x = jax.random.normal(ks[0], (BATCH, SEQ, D), f32)
    offset = jax.random.randint(ks[1], (BATCH, 1), 0, 4096, dtype=jnp.int32)
    positions = (offset + jnp.arange(SEQ, dtype=jnp.int32)[None, :]).astype(jnp.int32)
    return {
        "x": x,
        "positions": positions,
        "norm_g": gain(ks[2], (L, D)),
        "w_in": nrm(ks[3], (L, D, D_IN), D ** -0.5),
        "mla_q_norm_g": gain(ks[4], (L, MLA_Q_RANK)),
        "mla_w_uq": nrm(ks[5], (L, MLA_Q_RANK, MLA_HEADS * MLA_QK), MLA_Q_RANK ** -0.5),
        "mla_kv_norm_g": gain(ks[6], (L, MLA_KV_RANK)),
        "mla_w_ukv": nrm(ks[7], (L, MLA_KV_RANK, MLA_HEADS * (MLA_NOPE + MLA_V)), MLA_KV_RANK ** -0.5),
        "mla_q_g": gain(ks[8], (L, MLA_QK)),
        "mla_k_nope_g": gain(ks[9], (L, MLA_NOPE)),
        "mla_k_rope_g": gain(ks[10], (L, MLA_ROPE)),
        "conv_w": nrm(ks[11], (L, CONV_WIDTH, GW), CONV_WIDTH ** -0.5),
        "sg_ln_g": gain(ks[12], (L, GW)),
        "sg_ln_b": nrm(ks[13], (L, GW), 0.02),
        "sg_w": nrm(ks[14], (L, SG_GROUPS, SG_CHUNK, SG_CHUNK), SG_CHUNK ** -0.5),
        "sg_b": 1.0 + nrm(ks[15], (L, SG_GROUPS, SG_CHUNK), 0.1),
        "moba_q_g": gain(ks[16], (L, MOBA_HD)),
        "moba_k_g": gain(ks[17], (L, MOBA_HD)),
        "w_out": nrm(ks[18], (L, D_MIX, D), D_MIX ** -0.5),
    }


def reference(x, positions, norm_g, w_in, mla_q_norm_g, mla_w_uq, mla_kv_norm_g, mla_w_ukv,
              mla_q_g, mla_k_nope_g, mla_k_rope_g, conv_w, sg_ln_g, sg_ln_b, sg_w, sg_b,
              moba_q_g, moba_k_g, w_out):
    for l in range(DEPTH):
        x = x + hybrid_layer(x, positions, norm_g[l], w_in[l], mla_q_norm_g[l], mla_w_uq[l],
                             mla_kv_norm_g[l], mla_w_ukv[l], mla_q_g[l], mla_k_nope_g[l],
                             mla_k_rope_g[l], conv_w[l], sg_ln_g[l], sg_ln_b[l], sg_w[l], sg_b[l],
                             moba_q_g[l], moba_k_g[l], w_out[l])
    return x
```

```python
import functools

import jax
import jax.numpy as jnp
from jax import lax
from jax.experimental import pallas as pl
from jax.experimental.pallas import tpu as pltpu

D_MODEL = 1024
GW = 256
EPS = 1e-6
NEG = -1e30
ROPE_THETA = 500000.0

MLA_HEADS = 4
MLA_NOPE = 64
MLA_ROPE = 32
MLA_QK = MLA_NOPE + MLA_ROPE
MLA_V = 64
MLA_Q_RANK = 256
MLA_KV_RANK = 128

CONV_WIDTH = 3
SG_CHUNK = 128
SG_GROUPS = 4
SG_GDIM = 64

MOBA_HEADS = 4
MOBA_HD = 64
MOBA_ROT = 16
MOBA_BLOCK = 256
MOBA_TOPK = 3

N_HEADS = MLA_HEADS + MOBA_HEADS
DK = 128
DV = 64
BLK = MOBA_BLOCK

_SPLITS = (256, 128, 32, 256, 256, 256, 256, 256, 256, 256, 256, 256, 256, 256, 256)
_NAMES = ("cq", "ckv", "kr", "g_a", "c_h", "c_b", "c_c", "g_b", "sg_u", "sg_v", "g_c",
          "mq", "mk", "mv", "g_d")
_NAT = ("g_a", "c_h", "c_b", "c_c", "g_b", "sg_u", "sg_v", "g_c", "g_d")
_TR = ("cq", "ckv", "kr", "mq", "mk", "mv")
N_NAT = 9 * GW
N_TR = 256 + 128 + 32 + 3 * 256
TR_CQ, TR_CKV, TR_KR, TR_MQ, TR_MK, TR_MV = 0, 256, 384, 416, 672, 928

VMEM_LIMIT = 56 * 1024 * 1024


def _col_ranges():
    out, o = {}, 0
    for n, s in zip(_NAMES, _SPLITS):
        out[n] = (o, o + s)
        o += s
    return out


def _in_proj_kernel(x_ref, g_ref, wn_ref, wt_ref, pn_ref, pt_ref):
    x = x_ref[...]
    ms = jnp.mean(x * x, axis=-1, keepdims=True)
    h = (x * lax.rsqrt(ms + EPS) * g_ref[...]).astype(jnp.bfloat16)
    pn_ref[...] = jnp.dot(h, wn_ref[...], preferred_element_type=jnp.float32)
    pt_ref[...] = lax.dot_general(wt_ref[...], h, (((1,), (1,)), ((), ())),
                                  preferred_element_type=jnp.float32)


def _in_proj(x2, g, w_nat, w_tr, tm):
    s = x2.shape[0]
    return pl.pallas_call(
        _in_proj_kernel,
        grid=(s // tm,),
        in_specs=[
            pl.BlockSpec((tm, D_MODEL), lambda i: (i, 0)),
            pl.BlockSpec((1, D_MODEL), lambda i: (0, 0)),
            pl.BlockSpec((D_MODEL, N_NAT), lambda i: (0, 0)),
            pl.BlockSpec((N_TR, D_MODEL), lambda i: (0, 0)),
        ],
        out_specs=[
            pl.BlockSpec((tm, N_NAT), lambda i: (i, 0)),
            pl.BlockSpec((N_TR, tm), lambda i: (0, i)),
        ],
        out_shape=[
            jax.ShapeDtypeStruct((s, N_NAT), jnp.float32),
            jax.ShapeDtypeStruct((N_TR, s), jnp.float32),
        ],
        compiler_params=pltpu.CompilerParams(
            dimension_semantics=("arbitrary",), vmem_limit_bytes=VMEM_LIMIT),
        name="in_proj",
    )(x2, g, w_nat, w_tr)


def _rms_rows(v, gain):
    ms = jnp.mean(v * v, axis=0, keepdims=True)
    return v * lax.rsqrt(ms + EPS) * gain


def _rope_rows(v, cos, sin):
    half = v.shape[0] // 2
    x1, x2 = v[:half], v[half:]
    return x1 * cos - x2 * sin, x2 * cos + x1 * sin


def _prep_kernel(pt_ref, pos_ref, inv_a_ref, inv_d_ref, qng_ref, wuq_ref, kvng_ref, wukv_ref,
                 qg_ref, kng_ref, krg_ref, dqg_ref, dkg_ref,
                 q_ref, k_ref, v_ref, kmean_ref, *, tm):
    i = pl.program_id(0)
    nb_tile = tm // BLK

    @pl.when(i == 0)
    def _():
        kmean_ref[...] = jnp.zeros_like(kmean_ref)

    pos = pos_ref[...].astype(jnp.float32)
    ang_a = inv_a_ref[...] * pos
    cos_a, sin_a = jnp.cos(ang_a), jnp.sin(ang_a)
    ang_d = inv_d_ref[...] * pos
    cos_d, sin_d = jnp.cos(ang_d), jnp.sin(ang_d)

    zeros32 = jnp.zeros((DK - MLA_QK, tm), jnp.float32)

    cqn = _rms_rows(pt_ref[TR_CQ:TR_CQ + MLA_Q_RANK, :], qng_ref[...]).astype(jnp.bfloat16)
    q_all = jnp.dot(wuq_ref[...], cqn, preferred_element_type=jnp.float32)
    ckvn = _rms_rows(pt_ref[TR_CKV:TR_CKV + MLA_KV_RANK, :], kvng_ref[...]).astype(jnp.bfloat16)
    kv_all = jnp.dot(wukv_ref[...], ckvn, preferred_element_type=jnp.float32)
    krn = _rms_rows(pt_ref[TR_KR:TR_KR + MLA_ROPE, :], krg_ref[...])
    kr1, kr2 = _rope_rows(krn, cos_a, sin_a)
    scale_a = MLA_QK ** -0.5
    for h in range(MLA_HEADS):
        qn = _rms_rows(q_all[h * MLA_QK:(h + 1) * MLA_QK], qg_ref[...])
        r1, r2 = _rope_rows(qn[MLA_NOPE:], cos_a, sin_a)
        q_ext = jnp.concatenate([qn[:MLA_NOPE], r1, r2, zeros32], axis=0) * scale_a
        q_ref[h] = q_ext.astype(jnp.bfloat16)
        base = h * (MLA_NOPE + MLA_V)
        kn = _rms_rows(kv_all[base:base + MLA_NOPE], kng_ref[...])
        k_ext = jnp.concatenate([kn, kr1, kr2, zeros32], axis=0)
        k_nat = k_ext.T.astype(jnp.bfloat16)
        vv = kv_all[base + MLA_NOPE:base + MLA_NOPE + MLA_V].astype(jnp.bfloat16)
        for b in range(nb_tile):
            k_ref[h, b] = k_nat[b * BLK:(b + 1) * BLK]
            v_ref[h, b] = vv[:, b * BLK:(b + 1) * BLK]

    nblk = kmean_ref.shape[1]
    j_iota = lax.broadcasted_iota(jnp.int32, (nblk, tm), 0)
    t_iota = lax.broadcasted_iota(jnp.int32, (1, tm), 1)
    qblk = i * nb_tile + t_iota // BLK
    onehot = (j_iota == qblk).astype(jnp.float32)
    past = j_iota < qblk
    row_iota = lax.broadcasted_iota(jnp.int32, (nblk, DK), 0)
    zeros64 = jnp.zeros((DK - MOBA_HD, tm), jnp.float32)
    scale_d = MOBA_HD ** -0.5
    for h in range(MOBA_HEADS):
        hh = MLA_HEADS + h
        kn = _rms_rows(pt_ref[TR_MK + h * MOBA_HD:TR_MK + (h + 1) * MOBA_HD, :], dkg_ref[...])
        k1, k2 = _rope_rows(kn[:MOBA_ROT], cos_d, sin_d)
        k_ext = jnp.concatenate([k1, k2, kn[MOBA_ROT:], onehot], axis=0)
        k_nat = k_ext.T
        kmean = kmean_ref[h]
        for b in range(nb_tile):
            kb = k_nat[b * BLK:(b + 1) * BLK]
            k_ref[hh, b] = kb.astype(jnp.bfloat16)
            row = jnp.mean(kb, axis=0, keepdims=True)
            kmean = jnp.where(row_iota == i * nb_tile + b, row, kmean)
        kmean_ref[h] = kmean
        vv = pt_ref[TR_MV + h * MOBA_HD:TR_MV + (h + 1) * MOBA_HD, :].astype(jnp.bfloat16)
        for b in range(nb_tile):
            v_ref[hh, b] = vv[:, b * BLK:(b + 1) * BLK]

        qn = _rms_rows(pt_ref[TR_MQ + h * MOBA_HD:TR_MQ + (h + 1) * MOBA_HD, :], dqg_ref[...])
        q1, q2 = _rope_rows(qn[:MOBA_ROT], cos_d, sin_d)
        qd = jnp.concatenate([q1, q2, qn[MOBA_ROT:]], axis=0) * scale_d
        gate = jnp.dot(kmean, jnp.concatenate([qd, zeros64], axis=0),
                       preferred_element_type=jnp.float32, precision=lax.Precision.HIGHEST)
        gm = jnp.where(past, gate, NEG)
        sel = j_iota == qblk
        for r in range(MOBA_TOPK):
            m = jnp.max(gm, axis=0, keepdims=True)
            cand = jnp.where(gm == m, j_iota, nblk)
            jmin = jnp.min(cand, axis=0, keepdims=True)
            pick = j_iota == jmin
            sel = sel | (pick & (qblk > r))
            gm = jnp.where(pick, -jnp.inf, gm)
        bias = jnp.where(sel, 0.0, NEG)
        q_ref[hh] = jnp.concatenate([qd, bias], axis=0).astype(jnp.bfloat16)


def _prep(pt, pos, inv_a, inv_d, qng, wuq_t, kvng, wukv_t, qg, kng, krg, dqg, dkg, tm):
    s = pt.shape[1]
    nblk = s // BLK
    assert nblk <= DK - MOBA_HD
    nb_tile = tm // BLK
    full = lambda a: pl.BlockSpec(a.shape, lambda i: (0,) * a.ndim)
    small = (inv_a, inv_d, qng, wuq_t, kvng, wukv_t, qg, kng, krg, dqg, dkg)
    return pl.pallas_call(
        functools.partial(_prep_kernel, tm=tm),
        grid=(s // tm,),
        in_specs=[pl.BlockSpec((N_TR, tm), lambda i: (0, i)),
                  pl.BlockSpec((1, tm), lambda i: (0, i))] + [full(a) for a in small],
        out_specs=[
            pl.BlockSpec((N_HEADS, DK, tm), lambda i: (0, 0, i)),
            pl.BlockSpec((N_HEADS, nb_tile, BLK, DK), lambda i: (0, i, 0, 0)),
            pl.BlockSpec((N_HEADS, nb_tile, DV, BLK), lambda i: (0, i, 0, 0)),
        ],
        out_shape=[
            jax.ShapeDtypeStruct((N_HEADS, DK, s), jnp.bfloat16),
            jax.ShapeDtypeStruct((N_HEADS, nblk, BLK, DK), jnp.bfloat16),
            jax.ShapeDtypeStruct((N_HEADS, nblk, DV, BLK), jnp.bfloat16),
        ],
        scratch_shapes=[pltpu.VMEM((MOBA_HEADS, DK - MOBA_HD, DK), jnp.float32)],
        compiler_params=pltpu.CompilerParams(
            dimension_semantics=("arbitrary",), vmem_limit_bytes=VMEM_LIMIT),
        name="prep",
    )(pt, pos, *small)


def _attn_kernel(q_ref, k_ref, v_ref, o_ref):
    i = pl.program_id(1)
    q = q_ref[...]

    def tile(j, carry, diagonal):
        m, l, acc = carry
        s = jnp.dot(k_ref[j], q, preferred_element_type=jnp.float32)
        if diagonal:
            key_i = lax.broadcasted_iota(jnp.int32, (BLK, BLK), 0)
            qry_i = lax.broadcasted_iota(jnp.int32, (BLK, BLK), 1)
            s = jnp.where(key_i <= qry_i, s, NEG)
        m_new = jnp.maximum(m, jnp.max(s, axis=0, keepdims=True))
        alpha = jnp.exp(m - m_new)
        p = jnp.exp(s - m_new)
        l = alpha * l + jnp.sum(p, axis=0, keepdims=True)
        acc = alpha * acc + jnp.dot(v_ref[j], p.astype(jnp.bfloat16),
                                    preferred_element_type=jnp.float32)
        return m_new, l, acc

    init = (jnp.full((1, BLK), -jnp.inf, jnp.float32),
            jnp.zeros((1, BLK), jnp.float32),
            jnp.zeros((DV, BLK), jnp.float32))
    carry = lax.fori_loop(0, i, lambda j, c: tile(j, c, False), init)
    m, l, acc = tile(i, carry, True)
    o_ref[...] = acc / l


def _attn(q, k, v):
    nh, _, s = q.shape
    nblk = s // BLK
    return pl.pallas_call(
        _attn_kernel,
        grid=(nh, nblk),
        in_specs=[
            pl.BlockSpec((None, DK, BLK), lambda h, i: (h, 0, i)),
            pl.BlockSpec((None, nblk, BLK, DK), lambda h, i: (h, 0, 0, 0)),
            pl.BlockSpec((None, nblk, DV, BLK), lambda h, i: (h, 0, 0, 0)),
        ],
        out_specs=pl.BlockSpec((None, DV, BLK), lambda h, i: (h, 0, i)),
        out_shape=jax.ShapeDtypeStruct((nh, DV, s), jnp.float32),
        compiler_params=pltpu.CompilerParams(
            dimension_semantics=("arbitrary", "arbitrary"), vmem_limit_bytes=VMEM_LIMIT),
        name="attn",
    )(q, k, v)


def _silu(g):
    return g * (1.0 / (1.0 + jnp.exp(-g)))


def _out_proj_kernel(pn_ref, o_ref, x_ref, cw_ref, lng_ref, lnb_ref, sgw_ref, sgb_ref, wo_ref,
                     y_ref, zbuf_ref, *, tm):
    i = pl.program_id(0)
    HALO = 8

    def col(k):
        return pn_ref[:, k * GW:(k + 1) * GW]

    g_a, c_h, c_b, c_c, g_b, sg_u, sg_v, g_c, g_d = (col(k) for k in range(9))

    @pl.when(i == 0)
    def _():
        zbuf_ref[0:HALO, :] = jnp.zeros((HALO, GW), jnp.float32)

    z = c_c * c_h
    zbuf_ref[HALO:HALO + tm, :] = z
    z1 = zbuf_ref[HALO - 1:HALO - 1 + tm, :]
    z2 = zbuf_ref[HALO - 2:HALO - 2 + tm, :]
    cw = cw_ref[...]
    o_b = c_b * (cw[0:1] * z2 + cw[1:2] * z1 + cw[2:3] * z)
    zbuf_ref[0:HALO, :] = z[tm - HALO:tm]

    mu = jnp.mean(sg_v, axis=-1, keepdims=True)
    dv = sg_v - mu
    var = jnp.mean(dv * dv, axis=-1, keepdims=True)
    vn = (dv * lax.rsqrt(var + EPS) * lng_ref[...] + lnb_ref[...]).astype(jnp.bfloat16)
    t_i = lax.broadcasted_iota(jnp.int32, (SG_CHUNK, SG_CHUNK), 0)
    s_i = lax.broadcasted_iota(jnp.int32, (SG_CHUNK, SG_CHUNK), 1)
    lane_grp = lax.broadcasted_iota(jnp.int32, (SG_CHUNK, GW), 1) // SG_GDIM
    ws = [jnp.where(s_i <= t_i, sgw_ref[g], 0.0).astype(jnp.bfloat16) for g in range(SG_GROUPS)]
    mixed = []
    for c in range(tm // SG_CHUNK):
        vc = vn[c * SG_CHUNK:(c + 1) * SG_CHUNK]
        mc = jnp.dot(ws[0], vc, preferred_element_type=jnp.float32)
        for g in range(1, SG_GROUPS):
            mg = jnp.dot(ws[g], vc, preferred_element_type=jnp.float32)
            mc = jnp.where(lane_grp == g, mg, mc)
        mixed.append(mc + sgb_ref[...])
    o_c = sg_u * jnp.concatenate(mixed, axis=0)

    o_a = jnp.concatenate([o_ref[h] for h in range(MLA_HEADS)], axis=0).T
    o_d = jnp.concatenate([o_ref[MLA_HEADS + h] for h in range(MOBA_HEADS)], axis=0).T

    y = jnp.concatenate([o_a * _silu(g_a), o_b * _silu(g_b), o_c * _silu(g_c), o_d * _silu(g_d)],
                        axis=-1).astype(jnp.bfloat16)
    y_ref[...] = x_ref[...] + jnp.dot(y, wo_ref[...], preferred_element_type=jnp.float32)


def _out_proj(pn, o, x2, conv_w, lng, lnb, sg_w, sgb_full, w_out, tm):
    s = x2.shape[0]
    full = lambda a: pl.BlockSpec(a.shape, lambda i: (0,) * a.ndim)
    small = (conv_w, lng, lnb, sg_w, sgb_full, w_out)
    return pl.pallas_call(
        functools.partial(_out_proj_kernel, tm=tm),
        grid=(s // tm,),
        in_specs=[pl.BlockSpec((tm, N_NAT), lambda i: (i, 0)),
                  pl.BlockSpec((N_HEADS, DV, tm), lambda i: (0, 0, i)),
                  pl.BlockSpec((tm, D_MODEL), lambda i: (i, 0))] + [full(a) for a in small],
        out_specs=pl.BlockSpec((tm, D_MODEL), lambda i: (i, 0)),
        out_shape=jax.ShapeDtypeStruct((s, D_MODEL), jnp.float32),
        scratch_shapes=[pltpu.VMEM((tm + 8, GW), jnp.float32)],
        compiler_params=pltpu.CompilerParams(
            dimension_semantics=("arbitrary",), vmem_limit_bytes=VMEM_LIMIT),
        name="out_proj",
    )(pn, o, x2, *small)


def _rope_inv(rd):
    half = rd // 2
    inv = jnp.power(ROPE_THETA, -jnp.arange(half, dtype=jnp.float32) * 2.0 / rd)
    return inv.reshape(half, 1)


def _layer(x2, pos, norm_g, w_in, mla_q_norm_g, mla_w_uq, mla_kv_norm_g, mla_w_ukv, mla_q_g,
           mla_k_nope_g, mla_k_rope_g, conv_w, sg_ln_g, sg_ln_b, sg_w, sg_b, moba_q_g, moba_k_g,
           w_out, tm):
    cr = _col_ranges()
    bf = jnp.bfloat16
    w_nat = jnp.concatenate([w_in[:, cr[n][0]:cr[n][1]] for n in _NAT], axis=1).astype(bf)
    w_tr = jnp.concatenate([w_in[:, cr[n][0]:cr[n][1]] for n in _TR], axis=1).T.astype(bf)
    col = lambda a: a.reshape(-1, 1)
    pn, pt = _in_proj(x2, norm_g.reshape(1, -1), w_nat, w_tr, tm)
    q, k, v = _prep(pt, pos, _rope_inv(MLA_ROPE), _rope_inv(MOBA_ROT),
                    col(mla_q_norm_g), mla_w_uq.T.astype(bf), col(mla_kv_norm_g),
                    mla_w_ukv.T.astype(bf), col(mla_q_g), col(mla_k_nope_g), col(mla_k_rope_g),
                    col(moba_q_g), col(moba_k_g), tm)
    o = _attn(q, k, v)
    sgb_full = jnp.repeat(sg_b.T, SG_GDIM, axis=1)
    return _out_proj(pn, o, x2, conv_w, sg_ln_g.reshape(1, -1), sg_ln_b.reshape(1, -1), sg_w,
                     sgb_full, w_out.astype(bf), tm)


def kernel(x, positions, norm_g, w_in, mla_q_norm_g, mla_w_uq, mla_kv_norm_g, mla_w_ukv, mla_q_g, mla_k_nope_g, mla_k_rope_g, conv_w, sg_ln_g, sg_ln_b, sg_w, sg_b, moba_q_g, moba_k_g, w_out):
    b, s, d = x.shape
    assert b == 1 and d == D_MODEL and s % 512 == 0
    tm = 512
    x2 = x.reshape(s, d)
    pos = positions.reshape(1, s)
    params = (norm_g, w_in, mla_q_norm_g, mla_w_uq, mla_kv_norm_g, mla_w_ukv, mla_q_g,
              mla_k_nope_g, mla_k_rope_g, conv_w, sg_ln_g, sg_ln_b, sg_w, sg_b, moba_q_g,
              moba_k_g, w_out)
    for layer in range(norm_g.shape[0]):
        x2 = _layer(x2, pos, *(p[layer] for p in params), tm)
    return x2.reshape(b, s, d)
```

```python
import functools

import jax
import jax.numpy as jnp
from jax import lax
from jax.experimental import pallas as pl
from jax.experimental.pallas import tpu as pltpu

D_MODEL = 1024
GW = 256
EPS = 1e-6
NEG = -1e30
ROPE_THETA = 500000.0

MLA_HEADS = 4
MLA_NOPE = 64
MLA_ROPE = 32
MLA_QK = MLA_NOPE + MLA_ROPE
MLA_V = 64
MLA_Q_RANK = 256
MLA_KV_RANK = 128

CONV_WIDTH = 3
SG_CHUNK = 128
SG_GROUPS = 4
SG_GDIM = 64

MOBA_HEADS = 4
MOBA_HD = 64
MOBA_ROT = 16
MOBA_BLOCK = 256
MOBA_TOPK = 3

N_HEADS = MLA_HEADS + MOBA_HEADS
DK = 128
DV = 64
BLK = MOBA_BLOCK

_SPLITS = (256, 128, 32, 256, 256, 256, 256, 256, 256, 256, 256, 256, 256, 256, 256)
_NAMES = ("cq", "ckv", "kr", "g_a", "c_h", "c_b", "c_c", "g_b", "sg_u", "sg_v", "g_c",
          "mq", "mk", "mv", "g_d")
_NAT = ("g_a", "c_h", "c_b", "c_c", "g_b", "sg_u", "sg_v", "g_c", "g_d")
_TR = ("cq", "ckv", "kr", "mq", "mk", "mv")
N_NAT = 9 * GW
N_TR = 256 + 128 + 32 + 3 * 256
TR_CQ, TR_CKV, TR_KR, TR_MQ, TR_MK, TR_MV = 0, 256, 384, 416, 672, 928

VMEM_LIMIT = 56 * 1024 * 1024


def _col_ranges():
    out, o = {}, 0
    for n, s in zip(_NAMES, _SPLITS):
        out[n] = (o, o + s)
        o += s
    return out


def _in_proj_kernel(x_ref, g_ref, wn_ref, wt_ref, pn_ref, pt_ref):
    x = x_ref[...]
    ms = jnp.mean(x * x, axis=-1, keepdims=True)
    h = (x * lax.rsqrt(ms + EPS) * g_ref[...]).astype(jnp.bfloat16)
    pn_ref[...] = jnp.dot(h, wn_ref[...], preferred_element_type=jnp.float32)
    pt_ref[...] = lax.dot_general(wt_ref[...], h, (((1,), (1,)), ((), ())),
                                  preferred_element_type=jnp.float32)


def _in_proj(x2, g, w_nat, w_tr, tm):
    s = x2.shape[0]
    return pl.pallas_call(
        _in_proj_kernel,
        grid=(s // tm,),
        in_specs=[
            pl.BlockSpec((tm, D_MODEL), lambda i: (i, 0)),
            pl.BlockSpec((1, D_MODEL), lambda i: (0, 0)),
            pl.BlockSpec((D_MODEL, N_NAT), lambda i: (0, 0)),
            pl.BlockSpec((N_TR, D_MODEL), lambda i: (0, 0)),
        ],
        out_specs=[
            pl.BlockSpec((tm, N_NAT), lambda i: (i, 0)),
            pl.BlockSpec((N_TR, tm), lambda i: (0, i)),
        ],
        out_shape=[
            jax.ShapeDtypeStruct((s, N_NAT), jnp.float32),
            jax.ShapeDtypeStruct((N_TR, s), jnp.float32),
        ],
        compiler_params=pltpu.CompilerParams(
            dimension_semantics=("arbitrary",), vmem_limit_bytes=VMEM_LIMIT),
        name="in_proj",
    )(x2, g, w_nat, w_tr)


def _rms_rows(v, gain):
    ms = jnp.mean(v * v, axis=0, keepdims=True)
    return v * lax.rsqrt(ms + EPS) * gain


def _rope_rows(v, cos, sin):
    half = v.shape[0] // 2
    x1, x2 = v[:half], v[half:]
    return x1 * cos - x2 * sin, x2 * cos + x1 * sin


def _prep_kernel(pt_ref, pos_ref, inv_a_ref, inv_d_ref, qng_ref, wuq_ref, kvng_ref, wukv_ref,
                 qg_ref, kng_ref, krg_ref, dqg_ref, dkg_ref,
                 q_ref, k_ref, v_ref, kmean_ref, *, tm):
    i = pl.program_id(0)
    nb_tile = tm // BLK

    @pl.when(i == 0)
    def _():
        kmean_ref[...] = jnp.zeros_like(kmean_ref)

    pos = pos_ref[...].astype(jnp.float32)
    ang_a = inv_a_ref[...] * pos
    cos_a, sin_a = jnp.cos(ang_a), jnp.sin(ang_a)
    ang_d = inv_d_ref[...] * pos
    cos_d, sin_d = jnp.cos(ang_d), jnp.sin(ang_d)

    zeros32 = jnp.zeros((DK - MLA_QK, tm), jnp.float32)

    cqn = _rms_rows(pt_ref[TR_CQ:TR_CQ + MLA_Q_RANK, :], qng_ref[...]).astype(jnp.bfloat16)
    q_all = jnp.dot(wuq_ref[...], cqn, preferred_element_type=jnp.float32)
    ckvn = _rms_rows(pt_ref[TR_CKV:TR_CKV + MLA_KV_RANK, :], kvng_ref[...]).astype(jnp.bfloat16)
    kv_all = jnp.dot(wukv_ref[...], ckvn, preferred_element_type=jnp.float32)
    krn = _rms_rows(pt_ref[TR_KR:TR_KR + MLA_ROPE, :], krg_ref[...])
    kr1, kr2 = _rope_rows(krn, cos_a, sin_a)
    scale_a = MLA_QK ** -0.5
    for h in range(MLA_HEADS):
        qn = _rms_rows(q_all[h * MLA_QK:(h + 1) * MLA_QK], qg_ref[...])
        r1, r2 = _rope_rows(qn[MLA_NOPE:], cos_a, sin_a)
        q_ext = jnp.concatenate([qn[:MLA_NOPE], r1, r2, zeros32], axis=0) * scale_a
        q_ref[h] = q_ext.astype(jnp.bfloat16)
        base = h * (MLA_NOPE + MLA_V)
        kn = _rms_rows(kv_all[base:base + MLA_NOPE], kng_ref[...])
        k_ext = jnp.concatenate([kn, kr1, kr2, zeros32], axis=0)
        k_nat = k_ext.T.astype(jnp.bfloat16)
        vv = kv_all[base + MLA_NOPE:base + MLA_NOPE + MLA_V].astype(jnp.bfloat16)
        for b in range(nb_tile):
            k_ref[h, b] = k_nat[b * BLK:(b + 1) * BLK]
            v_ref[h, b] = vv[:, b * BLK:(b + 1) * BLK]

    nblk = kmean_ref.shape[1]
    j_iota = lax.broadcasted_iota(jnp.int32, (nblk, tm), 0)
    t_iota = lax.broadcasted_iota(jnp.int32, (1, tm), 1)
    qblk = i * nb_tile + t_iota // BLK
    onehot = (j_iota == qblk).astype(jnp.float32)
    past = j_iota < qblk
    row_iota = lax.broadcasted_iota(jnp.int32, (nblk, DK), 0)
    zeros64 = jnp.zeros((DK - MOBA_HD, tm), jnp.float32)
    scale_d = MOBA_HD ** -0.5
    for h in range(MOBA_HEADS):
        hh = MLA_HEADS + h
        kn = _rms_rows(pt_ref[TR_MK + h * MOBA_HD:TR_MK + (h + 1) * MOBA_HD, :], dkg_ref[...])
        k1, k2 = _rope_rows(kn[:MOBA_ROT], cos_d, sin_d)
        k_ext = jnp.concatenate([k1, k2, kn[MOBA_ROT:], onehot], axis=0)
        k_nat = k_ext.T
        kmean = kmean_ref[h]
        for b in range(nb_tile):
            kb = k_nat[b * BLK:(b + 1) * BLK]
            k_ref[hh, b] = kb.astype(jnp.bfloat16)
            row = jnp.mean(kb, axis=0, keepdims=True)
            kmean = jnp.where(row_iota == i * nb_tile + b, row, kmean)
        kmean_ref[h] = kmean
        vv = pt_ref[TR_MV + h * MOBA_HD:TR_MV + (h + 1) * MOBA_HD, :].astype(jnp.bfloat16)
        for b in range(nb_tile):
            v_ref[hh, b] = vv[:, b * BLK:(b + 1) * BLK]

        qn = _rms_rows(pt_ref[TR_MQ + h * MOBA_HD:TR_MQ + (h + 1) * MOBA_HD, :], dqg_ref[...])
        q1, q2 = _rope_rows(qn[:MOBA_ROT], cos_d, sin_d)
        qd = jnp.concatenate([q1, q2, qn[MOBA_ROT:]], axis=0) * scale_d
        gate = jnp.dot(kmean, jnp.concatenate([qd, zeros64], axis=0),
                       preferred_element_type=jnp.float32, precision=lax.Precision.HIGHEST)
        gm = jnp.where(past, gate, NEG)
        sel = j_iota == qblk
        for r in range(MOBA_TOPK):
            m = jnp.max(gm, axis=0, keepdims=True)
            cand = jnp.where(gm == m, j_iota, nblk)
            jmin = jnp.min(cand, axis=0, keepdims=True)
            pick = j_iota == jmin
            sel = sel | (pick & (qblk > r))
            gm = jnp.where(pick, -jnp.inf, gm)
        bias = jnp.where(sel, 0.0, NEG)
        q_ref[hh] = jnp.concatenate([qd, bias], axis=0).astype(jnp.bfloat16)


def _prep(pt, pos, inv_a, inv_d, qng, wuq_t, kvng, wukv_t, qg, kng, krg, dqg, dkg, tm):
    s = pt.shape[1]
    nblk = s // BLK
    assert nblk <= DK - MOBA_HD
    nb_tile = tm // BLK
    full = lambda a: pl.BlockSpec(a.shape, lambda i: (0,) * a.ndim)
    small = (inv_a, inv_d, qng, wuq_t, kvng, wukv_t, qg, kng, krg, dqg, dkg)
    return pl.pallas_call(
        functools.partial(_prep_kernel, tm=tm),
        grid=(s // tm,),
        in_specs=[pl.BlockSpec((N_TR, tm), lambda i: (0, i)),
                  pl.BlockSpec((1, tm), lambda i: (0, i))] + [full(a) for a in small],
        out_specs=[
            pl.BlockSpec((N_HEADS, DK, tm), lambda i: (0, 0, i)),
            pl.BlockSpec((N_HEADS, nb_tile, BLK, DK), lambda i: (0, i, 0, 0)),
            pl.BlockSpec((N_HEADS, nb_tile, DV, BLK), lambda i: (0, i, 0, 0)),
        ],
        out_shape=[
            jax.ShapeDtypeStruct((N_HEADS, DK, s), jnp.bfloat16),
            jax.ShapeDtypeStruct((N_HEADS, nblk, BLK, DK), jnp.bfloat16),
            jax.ShapeDtypeStruct((N_HEADS, nblk, DV, BLK), jnp.bfloat16),
        ],
        scratch_shapes=[pltpu.VMEM((MOBA_HEADS, DK - MOBA_HD, DK), jnp.float32)],
        compiler_params=pltpu.CompilerParams(
            dimension_semantics=("arbitrary",), vmem_limit_bytes=VMEM_LIMIT),
        name="prep",
    )(pt, pos, *small)


ATTN_HG = 4


def _attn_kernel(q_ref, k_ref, v_ref, o_ref, s_ref, acc_ref, m_ref, l_ref):
    i = pl.program_id(1)

    def scores(h, j, slot):
        s_ref[slot, h] = jnp.dot(k_ref[h, j], q_ref[h],
                                 preferred_element_type=jnp.float32)

    def softmax_pv(h, j, slot, diagonal):
        m, l = m_ref[h], l_ref[h]
        s = s_ref[slot, h]
        if diagonal:
            key_i = lax.broadcasted_iota(jnp.int32, (BLK, BLK), 0)
            qry_i = lax.broadcasted_iota(jnp.int32, (BLK, BLK), 1)
            s = jnp.where(key_i <= qry_i, s, NEG)
        m_new = jnp.maximum(m, jnp.max(s, axis=0, keepdims=True))
        alpha = jnp.exp(m - m_new)
        p = jnp.exp(s - m_new)
        m_ref[h] = m_new
        l_ref[h] = alpha * l + jnp.sum(p, axis=0, keepdims=True)
        acc_ref[h] = alpha * acc_ref[h] + jnp.dot(v_ref[h, j], p.astype(jnp.bfloat16),
                                                  preferred_element_type=jnp.float32)

    def step(j, slot):
        for h in range(ATTN_HG):
            scores(h, j + 1, 1 - slot)
            softmax_pv(h, j, slot, False)

    def finish(slot):
        for h in range(ATTN_HG):
            softmax_pv(h, i, slot, True)
            o_ref[h] = acc_ref[h] / l_ref[h]

    m_ref[...] = jnp.full_like(m_ref, -jnp.inf)
    l_ref[...] = jnp.zeros_like(l_ref)
    acc_ref[...] = jnp.zeros_like(acc_ref)
    for h in range(ATTN_HG):
        scores(h, 0, 0)

    def pair(t, carry):
        step(2 * t, 0)
        step(2 * t + 1, 1)
        return carry

    lax.fori_loop(0, i // 2, pair, 0)
    odd = i % 2 == 1

    @pl.when(odd)
    def _():
        step(i - 1, 0)
        finish(1)

    @pl.when(jnp.logical_not(odd))
    def _():
        finish(0)


def _attn(q, k, v):
    nh, _, s = q.shape
    nblk = s // BLK
    hg = ATTN_HG
    resident = dict(pipeline_mode=pl.Buffered(1))
    return pl.pallas_call(
        _attn_kernel,
        grid=(nh // hg, nblk),
        in_specs=[
            pl.BlockSpec((hg, DK, BLK), lambda g, i: (g, 0, i)),
            pl.BlockSpec((hg, nblk, BLK, DK), lambda g, i: (g, 0, 0, 0), **resident),
            pl.BlockSpec((hg, nblk, DV, BLK), lambda g, i: (g, 0, 0, 0), **resident),
        ],
        out_specs=pl.BlockSpec((hg, DV, BLK), lambda g, i: (g, 0, i)),
        out_shape=jax.ShapeDtypeStruct((nh, DV, s), jnp.float32),
        scratch_shapes=[pltpu.VMEM((2, hg, BLK, BLK), jnp.float32),
                        pltpu.VMEM((hg, DV, BLK), jnp.float32),
                        pltpu.VMEM((hg, 1, BLK), jnp.float32),
                        pltpu.VMEM((hg, 1, BLK), jnp.float32)],
        compiler_params=pltpu.CompilerParams(
            dimension_semantics=("arbitrary", "arbitrary"), vmem_limit_bytes=VMEM_LIMIT),
        name="attn",
    )(q, k, v)


def _silu(g):
    return g * (1.0 / (1.0 + jnp.exp(-g)))


def _out_proj_kernel(pn_ref, o_ref, x_ref, cw_ref, lng_ref, lnb_ref, sgw_ref, sgb_ref, wo_ref,
                     y_ref, zbuf_ref, *, tm):
    i = pl.program_id(0)
    HALO = 8

    def col(k):
        return pn_ref[:, k * GW:(k + 1) * GW]

    g_a, c_h, c_b, c_c, g_b, sg_u, sg_v, g_c, g_d = (col(k) for k in range(9))

    @pl.when(i == 0)
    def _():
        zbuf_ref[0:HALO, :] = jnp.zeros((HALO, GW), jnp.float32)

    z = c_c * c_h
    zbuf_ref[HALO:HALO + tm, :] = z
    z1 = zbuf_ref[HALO - 1:HALO - 1 + tm, :]
    z2 = zbuf_ref[HALO - 2:HALO - 2 + tm, :]
    cw = cw_ref[...]
    o_b = c_b * (cw[0:1] * z2 + cw[1:2] * z1 + cw[2:3] * z)
    zbuf_ref[0:HALO, :] = z[tm - HALO:tm]

    mu = jnp.mean(sg_v, axis=-1, keepdims=True)
    dv = sg_v - mu
    var = jnp.mean(dv * dv, axis=-1, keepdims=True)
    vn = (dv * lax.rsqrt(var + EPS) * lng_ref[...] + lnb_ref[...]).astype(jnp.bfloat16)
    t_i = lax.broadcasted_iota(jnp.int32, (SG_CHUNK, SG_CHUNK), 0)
    s_i = lax.broadcasted_iota(jnp.int32, (SG_CHUNK, SG_CHUNK), 1)
    lane_grp = lax.broadcasted_iota(jnp.int32, (SG_CHUNK, GW), 1) // SG_GDIM
    ws = [jnp.where(s_i <= t_i, sgw_ref[g], 0.0).astype(jnp.bfloat16) for g in range(SG_GROUPS)]
    mixed = []
    for c in range(tm // SG_CHUNK):
        vc = vn[c * SG_CHUNK:(c + 1) * SG_CHUNK]
        mc = jnp.dot(ws[0], vc, preferred_element_type=jnp.float32)
        for g in range(1, SG_GROUPS):
            mg = jnp.dot(ws[g], vc, preferred_element_type=jnp.float32)
            mc = jnp.where(lane_grp == g, mg, mc)
        mixed.append(mc + sgb_ref[...])
    o_c = sg_u * jnp.concatenate(mixed, axis=0)

    o_a = jnp.concatenate([o_ref[h] for h in range(MLA_HEADS)], axis=0).T
    o_d = jnp.concatenate([o_ref[MLA_HEADS + h] for h in range(MOBA_HEADS)], axis=0).T

    y = jnp.concatenate([o_a * _silu(g_a), o_b * _silu(g_b), o_c * _silu(g_c), o_d * _silu(g_d)],
                        axis=-1).astype(jnp.bfloat16)
    y_ref[...] = x_ref[...] + jnp.dot(y, wo_ref[...], preferred_element_type=jnp.float32)


def _out_proj(pn, o, x2, conv_w, lng, lnb, sg_w, sgb_full, w_out, tm):
    s = x2.shape[0]
    full = lambda a: pl.BlockSpec(a.shape, lambda i: (0,) * a.ndim)
    small = (conv_w, lng, lnb, sg_w, sgb_full, w_out)
    return pl.pallas_call(
        functools.partial(_out_proj_kernel, tm=tm),
        grid=(s // tm,),
        in_specs=[pl.BlockSpec((tm, N_NAT), lambda i: (i, 0)),
                  pl.BlockSpec((N_HEADS, DV, tm), lambda i: (0, 0, i)),
                  pl.BlockSpec((tm, D_MODEL), lambda i: (i, 0))] + [full(a) for a in small],
        out_specs=pl.BlockSpec((tm, D_MODEL), lambda i: (i, 0)),
        out_shape=jax.ShapeDtypeStruct((s, D_MODEL), jnp.float32),
        scratch_shapes=[pltpu.VMEM((tm + 8, GW), jnp.float32)],
        compiler_params=pltpu.CompilerParams(
            dimension_semantics=("arbitrary",), vmem_limit_bytes=VMEM_LIMIT),
        name="out_proj",
    )(pn, o, x2, *small)


def _rope_inv(rd):
    half = rd // 2
    inv = jnp.power(ROPE_THETA, -jnp.arange(half, dtype=jnp.float32) * 2.0 / rd)
    return inv.reshape(half, 1)


def _layer(x2, pos, norm_g, w_in, mla_q_norm_g, mla_w_uq, mla_kv_norm_g, mla_w_ukv, mla_q_g,
           mla_k_nope_g, mla_k_rope_g, conv_w, sg_ln_g, sg_ln_b, sg_w, sg_b, moba_q_g, moba_k_g,
           w_out, tm):
    cr = _col_ranges()
    bf = jnp.bfloat16
    w_nat = jnp.concatenate([w_in[:, cr[n][0]:cr[n][1]] for n in _NAT], axis=1).astype(bf)
    w_tr = jnp.concatenate([w_in[:, cr[n][0]:cr[n][1]] for n in _TR], axis=1).T.astype(bf)
    col = lambda a: a.reshape(-1, 1)
    pn, pt = _in_proj(x2, norm_g.reshape(1, -1), w_nat, w_tr, tm)
    q, k, v = _prep(pt, pos, _rope_inv(MLA_ROPE), _rope_inv(MOBA_ROT),
                    col(mla_q_norm_g), mla_w_uq.T.astype(bf), col(mla_kv_norm_g),
                    mla_w_ukv.T.astype(bf), col(mla_q_g), col(mla_k_nope_g), col(mla_k_rope_g),
                    col(moba_q_g), col(moba_k_g), tm)
    o = _attn(q, k, v)
    sgb_full = jnp.repeat(sg_b.T, SG_GDIM, axis=1)
    return _out_proj(pn, o, x2, conv_w, sg_ln_g.reshape(1, -1), sg_ln_b.reshape(1, -1), sg_w,
                     sgb_full, w_out.astype(bf), tm)


def kernel(x, positions, norm_g, w_in, mla_q_norm_g, mla_w_uq, mla_kv_norm_g, mla_w_ukv, mla_q_g, mla_k_nope_g, mla_k_rope_g, conv_w, sg_ln_g, sg_ln_b, sg_w, sg_b, moba_q_g, moba_k_g, w_out):
    b, s, d = x.shape
    assert b == 1 and d == D_MODEL and s % 512 == 0
    tm = 512
    x2 = x.reshape(s, d)
    pos = positions.reshape(1, s)
    params = (norm_g, w_in, mla_q_norm_g, mla_w_uq, mla_kv_norm_g, mla_w_ukv, mla_q_g,
              mla_k_nope_g, mla_k_rope_g, conv_w, sg_ln_g, sg_ln_b, sg_w, sg_b, moba_q_g,
              moba_k_g, w_out)
    for layer in range(norm_g.shape[0]):
        x2 = _layer(x2, pos, *(p[layer] for p in params), tm)
    return x2.reshape(b, s, d)
```

```python
import functools

import jax
import jax.numpy as jnp
from jax import lax
from jax.experimental import pallas as pl
from jax.experimental.pallas import tpu as pltpu

D_MODEL = 1024
GW = 256
EPS = 1e-6
NEG = -1e30
ROPE_THETA = 500000.0

MLA_HEADS = 4
MLA_NOPE = 64
MLA_ROPE = 32
MLA_QK = MLA_NOPE + MLA_ROPE
MLA_V = 64
MLA_Q_RANK = 256
MLA_KV_RANK = 128

CONV_WIDTH = 3
SG_CHUNK = 128
SG_GROUPS = 4
SG_GDIM = 64

MOBA_HEADS = 4
MOBA_HD = 64
MOBA_ROT = 16
MOBA_BLOCK = 256
MOBA_TOPK = 3

N_HEADS = MLA_HEADS + MOBA_HEADS
DK = 128
DV = 64
DVX = DV + 16
LOG2E = 1.4426950408889634
BLK = MOBA_BLOCK

_SPLITS = (256, 128, 32, 256, 256, 256, 256, 256, 256, 256, 256, 256, 256, 256, 256)
_NAMES = ("cq", "ckv", "kr", "g_a", "c_h", "c_b", "c_c", "g_b", "sg_u", "sg_v", "g_c",
          "mq", "mk", "mv", "g_d")
_NAT = ("g_a", "c_h", "c_b", "c_c", "g_b", "sg_u", "sg_v", "g_c", "g_d")
_TR = ("cq", "ckv", "kr", "mq", "mk", "mv")
N_NAT = 9 * GW
N_TR = 256 + 128 + 32 + 3 * 256
TR_CQ, TR_CKV, TR_KR, TR_MQ, TR_MK, TR_MV = 0, 256, 384, 416, 672, 928

VMEM_LIMIT = 56 * 1024 * 1024


def _col_ranges():
    out, o = {}, 0
    for n, s in zip(_NAMES, _SPLITS):
        out[n] = (o, o + s)
        o += s
    return out


def _in_proj_kernel(x_ref, g_ref, wn_ref, wt_ref, pn_ref, pt_ref):
    x = x_ref[...]
    ms = jnp.mean(x * x, axis=-1, keepdims=True)
    h = (x * lax.rsqrt(ms + EPS) * g_ref[...]).astype(jnp.bfloat16)
    pn_ref[...] = jnp.dot(h, wn_ref[...], preferred_element_type=jnp.float32)
    pt_ref[...] = lax.dot_general(wt_ref[...], h, (((1,), (1,)), ((), ())),
                                  preferred_element_type=jnp.float32)


def _in_proj(x2, g, w_nat, w_tr, tm):
    s = x2.shape[0]
    return pl.pallas_call(
        _in_proj_kernel,
        grid=(s // tm,),
        in_specs=[
            pl.BlockSpec((tm, D_MODEL), lambda i: (i, 0)),
            pl.BlockSpec((1, D_MODEL), lambda i: (0, 0)),
            pl.BlockSpec((D_MODEL, N_NAT), lambda i: (0, 0)),
            pl.BlockSpec((N_TR, D_MODEL), lambda i: (0, 0)),
        ],
        out_specs=[
            pl.BlockSpec((tm, N_NAT), lambda i: (i, 0)),
            pl.BlockSpec((N_TR, tm), lambda i: (0, i)),
        ],
        out_shape=[
            jax.ShapeDtypeStruct((s, N_NAT), jnp.float32),
            jax.ShapeDtypeStruct((N_TR, s), jnp.float32),
        ],
        compiler_params=pltpu.CompilerParams(
            dimension_semantics=("arbitrary",), vmem_limit_bytes=VMEM_LIMIT),
        name="in_proj",
    )(x2, g, w_nat, w_tr)


def _rms_rows(v, gain):
    ms = jnp.mean(v * v, axis=0, keepdims=True)
    return v * lax.rsqrt(ms + EPS) * gain


def _rope_rows(v, cos, sin):
    half = v.shape[0] // 2
    x1, x2 = v[:half], v[half:]
    return x1 * cos - x2 * sin, x2 * cos + x1 * sin


def _prep_kernel(pt_ref, pos_ref, inv_a_ref, inv_d_ref, qng_ref, wuq_ref, kvng_ref, wukv_ref,
                 qg_ref, kng_ref, krg_ref, dqg_ref, dkg_ref,
                 q_ref, k_ref, v_ref, kmean_ref, *, tm):
    i = pl.program_id(0)
    nb_tile = tm // BLK

    @pl.when(i == 0)
    def _():
        kmean_ref[...] = jnp.zeros_like(kmean_ref)

    pos = pos_ref[...].astype(jnp.float32)
    ang_a = inv_a_ref[...] * pos
    cos_a, sin_a = jnp.cos(ang_a), jnp.sin(ang_a)
    ang_d = inv_d_ref[...] * pos
    cos_d, sin_d = jnp.cos(ang_d), jnp.sin(ang_d)

    zeros32 = jnp.zeros((DK - MLA_QK, tm), jnp.float32)
    ones_rows = jnp.ones((DVX - DV, BLK), jnp.bfloat16)

    cqn = _rms_rows(pt_ref[TR_CQ:TR_CQ + MLA_Q_RANK, :], qng_ref[...]).astype(jnp.bfloat16)
    q_all = jnp.dot(wuq_ref[...], cqn, preferred_element_type=jnp.float32)
    ckvn = _rms_rows(pt_ref[TR_CKV:TR_CKV + MLA_KV_RANK, :], kvng_ref[...]).astype(jnp.bfloat16)
    kv_all = jnp.dot(wukv_ref[...], ckvn, preferred_element_type=jnp.float32)
    krn = _rms_rows(pt_ref[TR_KR:TR_KR + MLA_ROPE, :], krg_ref[...])
    kr1, kr2 = _rope_rows(krn, cos_a, sin_a)
    scale_a = MLA_QK ** -0.5 * LOG2E
    for h in range(MLA_HEADS):
        qn = _rms_rows(q_all[h * MLA_QK:(h + 1) * MLA_QK], qg_ref[...])
        r1, r2 = _rope_rows(qn[MLA_NOPE:], cos_a, sin_a)
        q_ext = jnp.concatenate([qn[:MLA_NOPE], r1, r2, zeros32], axis=0) * scale_a
        q_ref[h] = q_ext.astype(jnp.bfloat16)
        base = h * (MLA_NOPE + MLA_V)
        kn = _rms_rows(kv_all[base:base + MLA_NOPE], kng_ref[...])
        k_ext = jnp.concatenate([kn, kr1, kr2, zeros32], axis=0)
        k_nat = k_ext.T.astype(jnp.bfloat16)
        vv = kv_all[base + MLA_NOPE:base + MLA_NOPE + MLA_V].astype(jnp.bfloat16)
        for b in range(nb_tile):
            k_ref[h, b] = k_nat[b * BLK:(b + 1) * BLK]
            v_ref[h, b, 0:DV, :] = vv[:, b * BLK:(b + 1) * BLK]
            v_ref[h, b, DV:DVX, :] = ones_rows

    nblk = kmean_ref.shape[1]
    j_iota = lax.broadcasted_iota(jnp.int32, (nblk, tm), 0)
    t_iota = lax.broadcasted_iota(jnp.int32, (1, tm), 1)
    qblk = i * nb_tile + t_iota // BLK
    onehot = (j_iota == qblk).astype(jnp.float32)
    past = j_iota < qblk
    row_iota = lax.broadcasted_iota(jnp.int32, (nblk, DK), 0)
    zeros64 = jnp.zeros((DK - MOBA_HD, tm), jnp.float32)
    scale_d = MOBA_HD ** -0.5 * LOG2E
    for h in range(MOBA_HEADS):
        hh = MLA_HEADS + h
        kn = _rms_rows(pt_ref[TR_MK + h * MOBA_HD:TR_MK + (h + 1) * MOBA_HD, :], dkg_ref[...])
        k1, k2 = _rope_rows(kn[:MOBA_ROT], cos_d, sin_d)
        k_ext = jnp.concatenate([k1, k2, kn[MOBA_ROT:], onehot], axis=0)
        k_nat = k_ext.T
        kmean = kmean_ref[h]
        for b in range(nb_tile):
            kb = k_nat[b * BLK:(b + 1) * BLK]
            k_ref[hh, b] = kb.astype(jnp.bfloat16)
            row = jnp.mean(kb, axis=0, keepdims=True)
            kmean = jnp.where(row_iota == i * nb_tile + b, row, kmean)
        kmean_ref[h] = kmean
        vv = pt_ref[TR_MV + h * MOBA_HD:TR_MV + (h + 1) * MOBA_HD, :].astype(jnp.bfloat16)
        for b in range(nb_tile):
            v_ref[hh, b, 0:DV, :] = vv[:, b * BLK:(b + 1) * BLK]
            v_ref[hh, b, DV:DVX, :] = ones_rows

        qn = _rms_rows(pt_ref[TR_MQ + h * MOBA_HD:TR_MQ + (h + 1) * MOBA_HD, :], dqg_ref[...])
        q1, q2 = _rope_rows(qn[:MOBA_ROT], cos_d, sin_d)
        qd = jnp.concatenate([q1, q2, qn[MOBA_ROT:]], axis=0) * scale_d
        gate = jnp.dot(kmean, jnp.concatenate([qd, zeros64], axis=0),
                       preferred_element_type=jnp.float32, precision=lax.Precision.HIGHEST)
        gm = jnp.where(past, gate, NEG)
        sel = j_iota == qblk
        for r in range(MOBA_TOPK):
            m = jnp.max(gm, axis=0, keepdims=True)
            cand = jnp.where(gm == m, j_iota, nblk)
            jmin = jnp.min(cand, axis=0, keepdims=True)
            pick = j_iota == jmin
            sel = sel | (pick & (qblk > r))
            gm = jnp.where(pick, -jnp.inf, gm)
        bias = jnp.where(sel, 0.0, NEG)
        q_ref[hh] = jnp.concatenate([qd, bias], axis=0).astype(jnp.bfloat16)


def _prep(pt, pos, inv_a, inv_d, qng, wuq_t, kvng, wukv_t, qg, kng, krg, dqg, dkg, tm):
    s = pt.shape[1]
    nblk = s // BLK
    assert nblk <= DK - MOBA_HD
    nb_tile = tm // BLK
    full = lambda a: pl.BlockSpec(a.shape, lambda i: (0,) * a.ndim)
    small = (inv_a, inv_d, qng, wuq_t, kvng, wukv_t, qg, kng, krg, dqg, dkg)
    return pl.pallas_call(
        functools.partial(_prep_kernel, tm=tm),
        grid=(s // tm,),
        in_specs=[pl.BlockSpec((N_TR, tm), lambda i: (0, i)),
                  pl.BlockSpec((1, tm), lambda i: (0, i))] + [full(a) for a in small],
        out_specs=[
            pl.BlockSpec((N_HEADS, DK, tm), lambda i: (0, 0, i)),
            pl.BlockSpec((N_HEADS, nb_tile, BLK, DK), lambda i: (0, i, 0, 0)),
            pl.BlockSpec((N_HEADS, nb_tile, DVX, BLK), lambda i: (0, i, 0, 0)),
        ],
        out_shape=[
            jax.ShapeDtypeStruct((N_HEADS, DK, s), jnp.bfloat16),
            jax.ShapeDtypeStruct((N_HEADS, nblk, BLK, DK), jnp.bfloat16),
            jax.ShapeDtypeStruct((N_HEADS, nblk, DVX, BLK), jnp.bfloat16),
        ],
        scratch_shapes=[pltpu.VMEM((MOBA_HEADS, DK - MOBA_HD, DK), jnp.float32)],
        compiler_params=pltpu.CompilerParams(
            dimension_semantics=("arbitrary",), vmem_limit_bytes=VMEM_LIMIT),
        name="prep",
    )(pt, pos, *small)


ATTN_HG = 4


ATTN_UNROLL = 4
STEP_ORDER = (("qk", 0), ("qk", 1), ("qk", 2), ("pv", 0),
              ("pv", 1), ("qk", 3), ("pv", 2), ("pv", 3))


def _attn_kernel(q_ref, k_ref, v_ref, o_ref, s_ref, acc_ref, m_ref):
    i = pl.program_id(1)

    def scores(h, j, slot):
        s_ref[slot, h] = jnp.dot(k_ref[h, j], q_ref[h],
                                 preferred_element_type=jnp.float32)

    def softmax_pv(h, j, slot, diagonal):
        m = m_ref[h]
        s = s_ref[slot, h]
        if diagonal:
            key_i = lax.broadcasted_iota(jnp.int32, (BLK, BLK), 0)
            qry_i = lax.broadcasted_iota(jnp.int32, (BLK, BLK), 1)
            s = jnp.where(key_i <= qry_i, s, NEG)
        m_new = jnp.maximum(m, jnp.max(s, axis=0, keepdims=True))
        alpha = jnp.exp2(m - m_new)
        p = jnp.exp2(s - m_new)
        m_ref[h] = m_new
        acc_ref[h] = alpha * acc_ref[h] + jnp.dot(v_ref[h, j], p.astype(jnp.bfloat16),
                                                  preferred_element_type=jnp.float32)

    def step(j, slot):
        for kind, h in STEP_ORDER:
            if kind == "qk":
                scores(h, j + 1, 1 - slot)
            else:
                softmax_pv(h, j, slot, False)

    def finish(slot):
        for h in range(ATTN_HG):
            softmax_pv(h, i, slot, True)
            o_ref[h] = acc_ref[h, 0:DV, :] / acc_ref[h, DV:DV + 1, :]

    m_ref[...] = jnp.full_like(m_ref, -jnp.inf)
    acc_ref[...] = jnp.zeros_like(acc_ref)
    for h in range(ATTN_HG):
        scores(h, 0, 0)

    def group(t, carry):
        for u in range(ATTN_UNROLL):
            step(ATTN_UNROLL * t + u, u % 2)
        return carry

    lax.fori_loop(0, i // ATTN_UNROLL, group, 0)
    base = (i // ATTN_UNROLL) * ATTN_UNROLL
    rem = i - base
    for u in range(ATTN_UNROLL - 1):
        @pl.when(rem > u)
        def _():
            step(base + u, u % 2)

    for par in range(2):
        @pl.when(rem % 2 == par)
        def _():
            finish(par)


def _attn(q, k, v):
    nh, _, s = q.shape
    nblk = s // BLK
    hg = ATTN_HG
    resident = dict(pipeline_mode=pl.Buffered(1))
    return pl.pallas_call(
        _attn_kernel,
        grid=(nh // hg, nblk),
        in_specs=[
            pl.BlockSpec((hg, DK, BLK), lambda g, i: (g, 0, i)),
            pl.BlockSpec((hg, nblk, BLK, DK), lambda g, i: (g, 0, 0, 0), **resident),
            pl.BlockSpec((hg, nblk, DVX, BLK), lambda g, i: (g, 0, 0, 0), **resident),
        ],
        out_specs=pl.BlockSpec((hg, DV, BLK), lambda g, i: (g, 0, i)),
        out_shape=jax.ShapeDtypeStruct((nh, DV, s), jnp.float32),
        scratch_shapes=[pltpu.VMEM((2, hg, BLK, BLK), jnp.float32),
                        pltpu.VMEM((hg, DVX, BLK), jnp.float32),
                        pltpu.VMEM((hg, 1, BLK), jnp.float32)],
        compiler_params=pltpu.CompilerParams(
            dimension_semantics=("arbitrary", "arbitrary"), vmem_limit_bytes=VMEM_LIMIT),
        name="attn",
    )(q, k, v)


def _silu(g):
    return g * (1.0 / (1.0 + jnp.exp(-g)))


def _out_proj_kernel(pn_ref, o_ref, x_ref, cw_ref, lng_ref, lnb_ref, sgw_ref, sgb_ref, wo_ref,
                     y_ref, zbuf_ref, *, tm):
    i = pl.program_id(0)
    HALO = 8

    def col(k):
        return pn_ref[:, k * GW:(k + 1) * GW]

    g_a, c_h, c_b, c_c, g_b, sg_u, sg_v, g_c, g_d = (col(k) for k in range(9))

    @pl.when(i == 0)
    def _():
        zbuf_ref[0:HALO, :] = jnp.zeros((HALO, GW), jnp.float32)

    z = c_c * c_h
    zbuf_ref[HALO:HALO + tm, :] = z
    z1 = zbuf_ref[HALO - 1:HALO - 1 + tm, :]
    z2 = zbuf_ref[HALO - 2:HALO - 2 + tm, :]
    cw = cw_ref[...]
    o_b = c_b * (cw[0:1] * z2 + cw[1:2] * z1 + cw[2:3] * z)
    zbuf_ref[0:HALO, :] = z[tm - HALO:tm]

    mu = jnp.mean(sg_v, axis=-1, keepdims=True)
    dv = sg_v - mu
    var = jnp.mean(dv * dv, axis=-1, keepdims=True)
    vn = (dv * lax.rsqrt(var + EPS) * lng_ref[...] + lnb_ref[...]).astype(jnp.bfloat16)
    t_i = lax.broadcasted_iota(jnp.int32, (SG_CHUNK, SG_CHUNK), 0)
    s_i = lax.broadcasted_iota(jnp.int32, (SG_CHUNK, SG_CHUNK), 1)
    lane_grp = lax.broadcasted_iota(jnp.int32, (SG_CHUNK, GW), 1) // SG_GDIM
    ws = [jnp.where(s_i <= t_i, sgw_ref[g], 0.0).astype(jnp.bfloat16) for g in range(SG_GROUPS)]
    mixed = []
    for c in range(tm // SG_CHUNK):
        vc = vn[c * SG_CHUNK:(c + 1) * SG_CHUNK]
        mc = jnp.dot(ws[0], vc, preferred_element_type=jnp.float32)
        for g in range(1, SG_GROUPS):
            mg = jnp.dot(ws[g], vc, preferred_element_type=jnp.float32)
            mc = jnp.where(lane_grp == g, mg, mc)
        mixed.append(mc + sgb_ref[...])
    o_c = sg_u * jnp.concatenate(mixed, axis=0)

    o_a = jnp.concatenate([o_ref[h] for h in range(MLA_HEADS)], axis=0).T
    o_d = jnp.concatenate([o_ref[MLA_HEADS + h] for h in range(MOBA_HEADS)], axis=0).T

    y = jnp.concatenate([o_a * _silu(g_a), o_b * _silu(g_b), o_c * _silu(g_c), o_d * _silu(g_d)],
                        axis=-1).astype(jnp.bfloat16)
    y_ref[...] = x_ref[...] + jnp.dot(y, wo_ref[...], preferred_element_type=jnp.float32)


def _out_proj(pn, o, x2, conv_w, lng, lnb, sg_w, sgb_full, w_out, tm):
    s = x2.shape[0]
    full = lambda a: pl.BlockSpec(a.shape, lambda i: (0,) * a.ndim)
    small = (conv_w, lng, lnb, sg_w, sgb_full, w_out)
    return pl.pallas_call(
        functools.partial(_out_proj_kernel, tm=tm),
        grid=(s // tm,),
        in_specs=[pl.BlockSpec((tm, N_NAT), lambda i: (i, 0)),
                  pl.BlockSpec((N_HEADS, DV, tm), lambda i: (0, 0, i)),
                  pl.BlockSpec((tm, D_MODEL), lambda i: (i, 0))] + [full(a) for a in small],
        out_specs=pl.BlockSpec((tm, D_MODEL), lambda i: (i, 0)),
        out_shape=jax.ShapeDtypeStruct((s, D_MODEL), jnp.float32),
        scratch_shapes=[pltpu.VMEM((tm + 8, GW), jnp.float32)],
        compiler_params=pltpu.CompilerParams(
            dimension_semantics=("arbitrary",), vmem_limit_bytes=VMEM_LIMIT),
        name="out_proj",
    )(pn, o, x2, *small)


def _rope_inv(rd):
    half = rd // 2
    inv = jnp.power(ROPE_THETA, -jnp.arange(half, dtype=jnp.float32) * 2.0 / rd)
    return inv.reshape(half, 1)


def _layer(x2, pos, norm_g, w_in, mla_q_norm_g, mla_w_uq, mla_kv_norm_g, mla_w_ukv, mla_q_g,
           mla_k_nope_g, mla_k_rope_g, conv_w, sg_ln_g, sg_ln_b, sg_w, sg_b, moba_q_g, moba_k_g,
           w_out, tm):
    cr = _col_ranges()
    bf = jnp.bfloat16
    w_nat = jnp.concatenate([w_in[:, cr[n][0]:cr[n][1]] for n in _NAT], axis=1).astype(bf)
    w_tr = jnp.concatenate([w_in[:, cr[n][0]:cr[n][1]] for n in _TR], axis=1).T.astype(bf)
    col = lambda a: a.reshape(-1, 1)
    pn, pt = _in_proj(x2, norm_g.reshape(1, -1), w_nat, w_tr, tm)
    q, k, v = _prep(pt, pos, _rope_inv(MLA_ROPE), _rope_inv(MOBA_ROT),
                    col(mla_q_norm_g), mla_w_uq.T.astype(bf), col(mla_kv_norm_g),
                    mla_w_ukv.T.astype(bf), col(mla_q_g), col(mla_k_nope_g), col(mla_k_rope_g),
                    col(moba_q_g), col(moba_k_g), tm)
    o = _attn(q, k, v)
    sgb_full = jnp.repeat(sg_b.T, SG_GDIM, axis=1)
    return _out_proj(pn, o, x2, conv_w, sg_ln_g.reshape(1, -1), sg_ln_b.reshape(1, -1), sg_w,
                     sgb_full, w_out.astype(bf), tm)


def kernel(x, positions, norm_g, w_in, mla_q_norm_g, mla_w_uq, mla_kv_norm_g, mla_w_ukv, mla_q_g, mla_k_nope_g, mla_k_rope_g, conv_w, sg_ln_g, sg_ln_b, sg_w, sg_b, moba_q_g, moba_k_g, w_out):
    b, s, d = x.shape
    assert b == 1 and d == D_MODEL and s % 512 == 0
    tm = 512
    x2 = x.reshape(s, d)
    pos = positions.reshape(1, s)
    params = (norm_g, w_in, mla_q_norm_g, mla_w_uq, mla_kv_norm_g, mla_w_ukv, mla_q_g,
              mla_k_nope_g, mla_k_rope_g, conv_w, sg_ln_g, sg_ln_b, sg_w, sg_b, moba_q_g,
              moba_k_g, w_out)
    for layer in range(norm_g.shape[0]):
        x2 = _layer(x2, pos, *(p[layer] for p in params), tm)
    return x2.reshape(b, s, d)
```

```python
import functools

import jax
import jax.numpy as jnp
from jax import lax
from jax.experimental import pallas as pl
from jax.experimental.pallas import tpu as pltpu

D_MODEL = 1024
GW = 256
EPS = 1e-6
NEG = -1e30
ROPE_THETA = 500000.0

MLA_HEADS = 4
MLA_NOPE = 64
MLA_ROPE = 32
MLA_QK = MLA_NOPE + MLA_ROPE
MLA_V = 64
MLA_Q_RANK = 256
MLA_KV_RANK = 128

CONV_WIDTH = 3
SG_CHUNK = 128
SG_GROUPS = 4
SG_GDIM = 64

MOBA_HEADS = 4
MOBA_HD = 64
MOBA_ROT = 16
MOBA_BLOCK = 256
MOBA_TOPK = 3

N_HEADS = MLA_HEADS + MOBA_HEADS
DK = 128
DV = 64
DVX = DV + 16
LOG2E = 1.4426950408889634
BLK = MOBA_BLOCK

_SPLITS = (256, 128, 32, 256, 256, 256, 256, 256, 256, 256, 256, 256, 256, 256, 256)
_NAMES = ("cq", "ckv", "kr", "g_a", "c_h", "c_b", "c_c", "g_b", "sg_u", "sg_v", "g_c",
          "mq", "mk", "mv", "g_d")
_NAT = ("g_a", "c_h", "c_b", "c_c", "g_b", "sg_u", "sg_v", "g_c", "g_d")
_TR = ("cq", "ckv", "kr", "mq", "mk", "mv")
N_NAT = 9 * GW
N_TR = 256 + 128 + 32 + 3 * 256
TR_CQ, TR_CKV, TR_KR, TR_MQ, TR_MK, TR_MV = 0, 256, 384, 416, 672, 928

VMEM_LIMIT = 56 * 1024 * 1024


def _col_ranges():
    out, o = {}, 0
    for n, s in zip(_NAMES, _SPLITS):
        out[n] = (o, o + s)
        o += s
    return out


def _in_proj_kernel(x_ref, g_ref, wn_ref, wt_ref, pn_ref, pt_ref):
    x = x_ref[...]
    ms = jnp.mean(x * x, axis=-1, keepdims=True)
    h = (x * lax.rsqrt(ms + EPS) * g_ref[...]).astype(jnp.bfloat16)
    pn_ref[...] = jnp.dot(h, wn_ref[...], preferred_element_type=jnp.float32)
    pt_ref[...] = lax.dot_general(wt_ref[...], h, (((1,), (1,)), ((), ())),
                                  preferred_element_type=jnp.float32)


def _in_proj(x2, g, w_nat, w_tr, tm):
    s = x2.shape[0]
    return pl.pallas_call(
        _in_proj_kernel,
        grid=(s // tm,),
        in_specs=[
            pl.BlockSpec((tm, D_MODEL), lambda i: (i, 0)),
            pl.BlockSpec((1, D_MODEL), lambda i: (0, 0)),
            pl.BlockSpec((D_MODEL, N_NAT), lambda i: (0, 0)),
            pl.BlockSpec((N_TR, D_MODEL), lambda i: (0, 0)),
        ],
        out_specs=[
            pl.BlockSpec((tm, N_NAT), lambda i: (i, 0)),
            pl.BlockSpec((N_TR, tm), lambda i: (0, i)),
        ],
        out_shape=[
            jax.ShapeDtypeStruct((s, N_NAT), jnp.float32),
            jax.ShapeDtypeStruct((N_TR, s), jnp.float32),
        ],
        compiler_params=pltpu.CompilerParams(
            dimension_semantics=("arbitrary",), vmem_limit_bytes=VMEM_LIMIT),
        name="in_proj",
    )(x2, g, w_nat, w_tr)


def _rms_rows(v, gain):
    ms = jnp.mean(v * v, axis=0, keepdims=True)
    return v * lax.rsqrt(ms + EPS) * gain


def _rope_rows(v, cos, sin):
    half = v.shape[0] // 2
    x1, x2 = v[:half], v[half:]
    return x1 * cos - x2 * sin, x2 * cos + x1 * sin


def _prep_kernel(pt_ref, pos_ref, inv_a_ref, inv_d_ref, qng_ref, wuq_ref, kvng_ref, wukv_ref,
                 qg_ref, kng_ref, krg_ref, dqg_ref, dkg_ref,
                 q_ref, k_ref, v_ref, kmean_ref, *, tm):
    i = pl.program_id(0)
    nb_tile = tm // BLK

    @pl.when(i == 0)
    def _():
        kmean_ref[...] = jnp.zeros_like(kmean_ref)

    pos = pos_ref[...].astype(jnp.float32)
    ang_a = inv_a_ref[...] * pos
    cos_a, sin_a = jnp.cos(ang_a), jnp.sin(ang_a)
    ang_d = inv_d_ref[...] * pos
    cos_d, sin_d = jnp.cos(ang_d), jnp.sin(ang_d)

    zeros32 = jnp.zeros((DK - MLA_QK, tm), jnp.float32)
    ones_rows = jnp.ones((DVX - DV, BLK), jnp.bfloat16)

    cqn = _rms_rows(pt_ref[TR_CQ:TR_CQ + MLA_Q_RANK, :], qng_ref[...]).astype(jnp.bfloat16)
    q_all = jnp.dot(wuq_ref[...], cqn, preferred_element_type=jnp.float32)
    ckvn = _rms_rows(pt_ref[TR_CKV:TR_CKV + MLA_KV_RANK, :], kvng_ref[...]).astype(jnp.bfloat16)
    kv_all = jnp.dot(wukv_ref[...], ckvn, preferred_element_type=jnp.float32)
    krn = _rms_rows(pt_ref[TR_KR:TR_KR + MLA_ROPE, :], krg_ref[...])
    kr1, kr2 = _rope_rows(krn, cos_a, sin_a)
    scale_a = MLA_QK ** -0.5 * LOG2E
    for h in range(MLA_HEADS):
        qn = _rms_rows(q_all[h * MLA_QK:(h + 1) * MLA_QK], qg_ref[...])
        r1, r2 = _rope_rows(qn[MLA_NOPE:], cos_a, sin_a)
        q_ext = jnp.concatenate([qn[:MLA_NOPE], r1, r2, zeros32], axis=0) * scale_a
        q_ref[h] = q_ext.astype(jnp.bfloat16)
        base = h * (MLA_NOPE + MLA_V)
        kn = _rms_rows(kv_all[base:base + MLA_NOPE], kng_ref[...])
        k_ext = jnp.concatenate([kn, kr1, kr2, zeros32], axis=0)
        k_nat = k_ext.T.astype(jnp.bfloat16)
        vv = kv_all[base + MLA_NOPE:base + MLA_NOPE + MLA_V].astype(jnp.bfloat16)
        for b in range(nb_tile):
            k_ref[h, b] = k_nat[b * BLK:(b + 1) * BLK]
            v_ref[h, b, 0:DV, :] = vv[:, b * BLK:(b + 1) * BLK]
            v_ref[h, b, DV:DVX, :] = ones_rows

    nblk = kmean_ref.shape[1]
    j_iota = lax.broadcasted_iota(jnp.int32, (nblk, tm), 0)
    t_iota = lax.broadcasted_iota(jnp.int32, (1, tm), 1)
    qblk = i * nb_tile + t_iota // BLK
    onehot = (j_iota == qblk).astype(jnp.float32)
    past = j_iota < qblk
    row_iota = lax.broadcasted_iota(jnp.int32, (nblk, DK), 0)
    zeros64 = jnp.zeros((DK - MOBA_HD, tm), jnp.float32)
    scale_d = MOBA_HD ** -0.5 * LOG2E
    for h in range(MOBA_HEADS):
        hh = MLA_HEADS + h
        kn = _rms_rows(pt_ref[TR_MK + h * MOBA_HD:TR_MK + (h + 1) * MOBA_HD, :], dkg_ref[...])
        k1, k2 = _rope_rows(kn[:MOBA_ROT], cos_d, sin_d)
        k_ext = jnp.concatenate([k1, k2, kn[MOBA_ROT:], onehot], axis=0)
        k_nat = k_ext.T
        kmean = kmean_ref[h]
        for b in range(nb_tile):
            kb = k_nat[b * BLK:(b + 1) * BLK]
            k_ref[hh, b] = kb.astype(jnp.bfloat16)
            row = jnp.mean(kb, axis=0, keepdims=True)
            kmean = jnp.where(row_iota == i * nb_tile + b, row, kmean)
        kmean_ref[h] = kmean
        vv = pt_ref[TR_MV + h * MOBA_HD:TR_MV + (h + 1) * MOBA_HD, :].astype(jnp.bfloat16)
        for b in range(nb_tile):
            v_ref[hh, b, 0:DV, :] = vv[:, b * BLK:(b + 1) * BLK]
            v_ref[hh, b, DV:DVX, :] = ones_rows

        qn = _rms_rows(pt_ref[TR_MQ + h * MOBA_HD:TR_MQ + (h + 1) * MOBA_HD, :], dqg_ref[...])
        q1, q2 = _rope_rows(qn[:MOBA_ROT], cos_d, sin_d)
        qd = jnp.concatenate([q1, q2, qn[MOBA_ROT:]], axis=0) * scale_d
        gate = jnp.dot(kmean, jnp.concatenate([qd, zeros64], axis=0),
                       preferred_element_type=jnp.float32, precision=lax.Precision.HIGHEST)
        gm = jnp.where(past, gate, NEG)
        sel = j_iota == qblk
        for r in range(MOBA_TOPK):
            m = jnp.max(gm, axis=0, keepdims=True)
            cand = jnp.where(gm == m, j_iota, nblk)
            jmin = jnp.min(cand, axis=0, keepdims=True)
            pick = j_iota == jmin
            sel = sel | (pick & (qblk > r))
            gm = jnp.where(pick, -jnp.inf, gm)
        bias = jnp.where(sel, 0.0, NEG)
        q_ref[hh] = jnp.concatenate([qd, bias], axis=0).astype(jnp.bfloat16)


def _prep(pt, pos, inv_a, inv_d, qng, wuq_t, kvng, wukv_t, qg, kng, krg, dqg, dkg, tm):
    s = pt.shape[1]
    nblk = s // BLK
    assert nblk <= DK - MOBA_HD
    nb_tile = tm // BLK
    full = lambda a: pl.BlockSpec(a.shape, lambda i: (0,) * a.ndim)
    small = (inv_a, inv_d, qng, wuq_t, kvng, wukv_t, qg, kng, krg, dqg, dkg)
    return pl.pallas_call(
        functools.partial(_prep_kernel, tm=tm),
        grid=(s // tm,),
        in_specs=[pl.BlockSpec((N_TR, tm), lambda i: (0, i)),
                  pl.BlockSpec((1, tm), lambda i: (0, i))] + [full(a) for a in small],
        out_specs=[
            pl.BlockSpec((N_HEADS, DK, tm), lambda i: (0, 0, i)),
            pl.BlockSpec((N_HEADS, nb_tile, BLK, DK), lambda i: (0, i, 0, 0)),
            pl.BlockSpec((N_HEADS, nb_tile, DVX, BLK), lambda i: (0, i, 0, 0)),
        ],
        out_shape=[
            jax.ShapeDtypeStruct((N_HEADS, DK, s), jnp.bfloat16),
            jax.ShapeDtypeStruct((N_HEADS, nblk, BLK, DK), jnp.bfloat16),
            jax.ShapeDtypeStruct((N_HEADS, nblk, DVX, BLK), jnp.bfloat16),
        ],
        scratch_shapes=[pltpu.VMEM((MOBA_HEADS, DK - MOBA_HD, DK), jnp.float32)],
        compiler_params=pltpu.CompilerParams(
            dimension_semantics=("arbitrary",), vmem_limit_bytes=VMEM_LIMIT),
        name="prep",
    )(pt, pos, *small)


ATTN_HG = 4
ATTN_TQ = BLK
N_DIAG = ATTN_TQ // BLK
ATTN_UNROLL_LONG = 16
ATTN_UNROLL = 4
STEP_ORDER = (("qk", 0), ("qk", 1), ("qk", 2), ("pv", 0),
              ("pv", 1), ("qk", 3), ("pv", 2), ("pv", 3))


def _attn_kernel(q_ref, k_ref, v_ref, o_ref, s_ref, acc_ref, m_ref):
    i = pl.program_id(1)
    n_full = i * N_DIAG

    def scores(h, j, slot):
        s_ref[slot, h] = jnp.dot(k_ref[h, j], q_ref[h],
                                 preferred_element_type=jnp.float32)

    def softmax_pv(h, j, slot, diag):
        m = m_ref[h]
        s = s_ref[slot, h]
        if diag is not None:
            key_i = lax.broadcasted_iota(jnp.int32, (BLK, ATTN_TQ), 0) + diag * BLK
            qry_i = lax.broadcasted_iota(jnp.int32, (BLK, ATTN_TQ), 1)
            s = jnp.where(key_i <= qry_i, s, NEG)
        m_new = jnp.maximum(m, jnp.max(s, axis=0, keepdims=True))
        alpha = jnp.exp2(m - m_new)
        p = jnp.exp2(s - m_new)
        m_ref[h] = m_new
        acc_ref[h] = alpha * acc_ref[h] + jnp.dot(v_ref[h, j], p.astype(jnp.bfloat16),
                                                  preferred_element_type=jnp.float32)

    def step(j, slot, diag=None, last=False):
        for kind, h in STEP_ORDER:
            if kind == "pv":
                softmax_pv(h, j, slot, diag)
            elif not last:
                scores(h, j + 1, 1 - slot)

    m_ref[...] = jnp.full_like(m_ref, -jnp.inf)
    acc_ref[...] = jnp.zeros_like(acc_ref)
    for h in range(ATTN_HG):
        scores(h, 0, 0)

    base = 0
    for unroll in (ATTN_UNROLL_LONG, ATTN_UNROLL):
        def group(t, carry, unroll=unroll, base=base):
            for u in range(unroll):
                step(base + unroll * t + u, u % 2)
            return carry

        n_groups = (n_full - base) // unroll
        lax.fori_loop(0, n_groups, group, 0)
        base = base + n_groups * unroll

    for r in range(0, ATTN_UNROLL, N_DIAG):
        @pl.when(n_full - base == r)
        def _():
            for u in range(r):
                step(base + u, u % 2)
            for d in range(N_DIAG):
                step(n_full + d, (r + d) % 2, diag=d, last=d == N_DIAG - 1)
            for h in range(ATTN_HG):
                o_ref[h] = (acc_ref[h, 0:DV, :] / acc_ref[h, DV:DV + 1, :]).astype(o_ref.dtype)


def _attn(q, k, v):
    nh, _, s = q.shape
    nblk = s // BLK
    hg, tq = ATTN_HG, ATTN_TQ
    assert ATTN_UNROLL % 2 == 0 and ATTN_UNROLL % N_DIAG == 0 and s % tq == 0
    resident = dict(pipeline_mode=pl.Buffered(1))
    return pl.pallas_call(
        _attn_kernel,
        grid=(nh // hg, s // tq),
        in_specs=[
            pl.BlockSpec((hg, DK, tq), lambda g, i: (g, 0, i)),
            pl.BlockSpec((hg, nblk, BLK, DK), lambda g, i: (g, 0, 0, 0), **resident),
            pl.BlockSpec((hg, nblk, DVX, BLK), lambda g, i: (g, 0, 0, 0), **resident),
        ],
        out_specs=pl.BlockSpec((hg, DV, tq), lambda g, i: (g, 0, i)),
        out_shape=jax.ShapeDtypeStruct((nh, DV, s), jnp.bfloat16),
        scratch_shapes=[pltpu.VMEM((2, hg, BLK, tq), jnp.float32),
                        pltpu.VMEM((hg, DVX, tq), jnp.float32),
                        pltpu.VMEM((hg, 1, tq), jnp.float32)],
        compiler_params=pltpu.CompilerParams(
            dimension_semantics=("arbitrary", "arbitrary"), vmem_limit_bytes=VMEM_LIMIT),
        name="attn",
    )(q, k, v)


def _silu(g):
    return g * (1.0 / (1.0 + jnp.exp(-g)))


def _out_proj_kernel(pn_ref, o_ref, x_ref, cw_ref, lng_ref, lnb_ref, sgw_ref, sgb_ref, wo_ref,
                     y_ref, zbuf_ref, *, tm):
    i = pl.program_id(0)
    HALO = 8

    def col(k):
        return pn_ref[:, k * GW:(k + 1) * GW]

    g_a, c_h, c_b, c_c, g_b, sg_u, sg_v, g_c, g_d = (col(k) for k in range(9))

    @pl.when(i == 0)
    def _():
        zbuf_ref[0:HALO, :] = jnp.zeros((HALO, GW), jnp.float32)

    z = c_c * c_h
    zbuf_ref[HALO:HALO + tm, :] = z
    z1 = zbuf_ref[HALO - 1:HALO - 1 + tm, :]
    z2 = zbuf_ref[HALO - 2:HALO - 2 + tm, :]
    cw = cw_ref[...]
    o_b = c_b * (cw[0:1] * z2 + cw[1:2] * z1 + cw[2:3] * z)
    zbuf_ref[0:HALO, :] = z[tm - HALO:tm]

    mu = jnp.mean(sg_v, axis=-1, keepdims=True)
    dv = sg_v - mu
    var = jnp.mean(dv * dv, axis=-1, keepdims=True)
    vn = (dv * lax.rsqrt(var + EPS) * lng_ref[...] + lnb_ref[...]).astype(jnp.bfloat16)
    t_i = lax.broadcasted_iota(jnp.int32, (SG_CHUNK, SG_CHUNK), 0)
    s_i = lax.broadcasted_iota(jnp.int32, (SG_CHUNK, SG_CHUNK), 1)
    lane_grp = lax.broadcasted_iota(jnp.int32, (SG_CHUNK, GW), 1) // SG_GDIM
    ws = [jnp.where(s_i <= t_i, sgw_ref[g], 0.0).astype(jnp.bfloat16) for g in range(SG_GROUPS)]
    mixed = []
    for c in range(tm // SG_CHUNK):
        vc = vn[c * SG_CHUNK:(c + 1) * SG_CHUNK]
        mc = jnp.dot(ws[0], vc, preferred_element_type=jnp.float32)
        for g in range(1, SG_GROUPS):
            mg = jnp.dot(ws[g], vc, preferred_element_type=jnp.float32)
            mc = jnp.where(lane_grp == g, mg, mc)
        mixed.append(mc + sgb_ref[...])
    o_c = sg_u * jnp.concatenate(mixed, axis=0)

    o_a = jnp.concatenate([o_ref[h].astype(jnp.float32) for h in range(MLA_HEADS)], axis=0).T
    o_d = jnp.concatenate([o_ref[MLA_HEADS + h].astype(jnp.float32)
                           for h in range(MOBA_HEADS)], axis=0).T

    y = jnp.concatenate([o_a * _silu(g_a), o_b * _silu(g_b), o_c * _silu(g_c), o_d * _silu(g_d)],
                        axis=-1).astype(jnp.bfloat16)
    y_ref[...] = x_ref[...] + jnp.dot(y, wo_ref[...], preferred_element_type=jnp.float32)


def _out_proj(pn, o, x2, conv_w, lng, lnb, sg_w, sgb_full, w_out, tm):
    s = x2.shape[0]
    full = lambda a: pl.BlockSpec(a.shape, lambda i: (0,) * a.ndim)
    small = (conv_w, lng, lnb, sg_w, sgb_full, w_out)
    return pl.pallas_call(
        functools.partial(_out_proj_kernel, tm=tm),
        grid=(s // tm,),
        in_specs=[pl.BlockSpec((tm, N_NAT), lambda i: (i, 0)),
                  pl.BlockSpec((N_HEADS, DV, tm), lambda i: (0, 0, i)),
                  pl.BlockSpec((tm, D_MODEL), lambda i: (i, 0))] + [full(a) for a in small],
        out_specs=pl.BlockSpec((tm, D_MODEL), lambda i: (i, 0)),
        out_shape=jax.ShapeDtypeStruct((s, D_MODEL), jnp.float32),
        scratch_shapes=[pltpu.VMEM((tm + 8, GW), jnp.float32)],
        compiler_params=pltpu.CompilerParams(
            dimension_semantics=("arbitrary",), vmem_limit_bytes=VMEM_LIMIT),
        name="out_proj",
    )(pn, o, x2, *small)


def _rope_inv(rd):
    half = rd // 2
    inv = jnp.power(ROPE_THETA, -jnp.arange(half, dtype=jnp.float32) * 2.0 / rd)
    return inv.reshape(half, 1)


def _layer(x2, pos, norm_g, w_in, mla_q_norm_g, mla_w_uq, mla_kv_norm_g, mla_w_ukv, mla_q_g,
           mla_k_nope_g, mla_k_rope_g, conv_w, sg_ln_g, sg_ln_b, sg_w, sg_b, moba_q_g, moba_k_g,
           w_out, tm):
    cr = _col_ranges()
    bf = jnp.bfloat16
    w_nat = jnp.concatenate([w_in[:, cr[n][0]:cr[n][1]] for n in _NAT], axis=1).astype(bf)
    w_tr = jnp.concatenate([w_in[:, cr[n][0]:cr[n][1]] for n in _TR], axis=1).T.astype(bf)
    col = lambda a: a.reshape(-1, 1)
    pn, pt = _in_proj(x2, norm_g.reshape(1, -1), w_nat, w_tr, tm)
    q, k, v = _prep(pt, pos, _rope_inv(MLA_ROPE), _rope_inv(MOBA_ROT),
                    col(mla_q_norm_g), mla_w_uq.T.astype(bf), col(mla_kv_norm_g),
                    mla_w_ukv.T.astype(bf), col(mla_q_g), col(mla_k_nope_g), col(mla_k_rope_g),
                    col(moba_q_g), col(moba_k_g), 2 * tm)
    o = _attn(q, k, v)
    sgb_full = jnp.repeat(sg_b.T, SG_GDIM, axis=1)
    return _out_proj(pn, o, x2, conv_w, sg_ln_g.reshape(1, -1), sg_ln_b.reshape(1, -1), sg_w,
                     sgb_full, w_out.astype(bf), 2 * tm)


def kernel(x, positions, norm_g, w_in, mla_q_norm_g, mla_w_uq, mla_kv_norm_g, mla_w_ukv, mla_q_g, mla_k_nope_g, mla_k_rope_g, conv_w, sg_ln_g, sg_ln_b, sg_w, sg_b, moba_q_g, moba_k_g, w_out):
    b, s, d = x.shape
    assert b == 1 and d == D_MODEL and s % 512 == 0
    tm = 512
    x2 = x.reshape(s, d)
    pos = positions.reshape(1, s)
    params = (norm_g, w_in, mla_q_norm_g, mla_w_uq, mla_kv_norm_g, mla_w_ukv, mla_q_g,
              mla_k_nope_g, mla_k_rope_g, conv_w, sg_ln_g, sg_ln_b, sg_w, sg_b, moba_q_g,
              moba_k_g, w_out)
    for layer in range(norm_g.shape[0]):
        x2 = _layer(x2, pos, *(p[layer] for p in params), tm)
    return x2.reshape(b, s, d)
```

```python
import functools

import jax
import jax.numpy as jnp
from jax import lax
from jax.experimental import pallas as pl
from jax.experimental.pallas import tpu as pltpu

D_MODEL = 1024
GW = 256
EPS = 1e-6
NEG = -1e30
ROPE_THETA = 500000.0

MLA_HEADS = 4
MLA_NOPE = 64
MLA_ROPE = 32
MLA_QK = MLA_NOPE + MLA_ROPE
MLA_V = 64
MLA_Q_RANK = 256
MLA_KV_RANK = 128

CONV_WIDTH = 3
SG_CHUNK = 128
SG_GROUPS = 4
SG_GDIM = 64

MOBA_HEADS = 4
MOBA_HD = 64
MOBA_ROT = 16
MOBA_BLOCK = 256
MOBA_TOPK = 3

N_HEADS = MLA_HEADS + MOBA_HEADS
DK = 128
DV = 64
DVX = DV + 16
LOG2E = 1.4426950408889634
BLK = MOBA_BLOCK

_SPLITS = (256, 128, 32, 256, 256, 256, 256, 256, 256, 256, 256, 256, 256, 256, 256)
_NAMES = ("cq", "ckv", "kr", "g_a", "c_h", "c_b", "c_c", "g_b", "sg_u", "sg_v", "g_c",
          "mq", "mk", "mv", "g_d")
_NAT = ("g_a", "c_h", "c_b", "c_c", "g_b", "sg_u", "sg_v", "g_c", "g_d")
_TR = ("cq", "ckv", "kr", "mq", "mk", "mv")
N_NAT = 9 * GW
N_TR = 256 + 128 + 32 + 3 * 256
TR_CQ, TR_CKV, TR_KR, TR_MQ, TR_MK, TR_MV = 0, 256, 384, 416, 672, 928

VMEM_LIMIT = 56 * 1024 * 1024


def _col_ranges():
    out, o = {}, 0
    for n, s in zip(_NAMES, _SPLITS):
        out[n] = (o, o + s)
        o += s
    return out


def _in_proj_kernel(x_ref, g_ref, wn_ref, wt_ref, pn_ref, pt_ref):
    x = x_ref[...]
    ms = jnp.mean(x * x, axis=-1, keepdims=True)
    h = (x * lax.rsqrt(ms + EPS) * g_ref[...]).astype(jnp.bfloat16)
    pn_ref[...] = jnp.dot(h, wn_ref[...], preferred_element_type=jnp.float32)
    pt_ref[...] = lax.dot_general(wt_ref[...], h, (((1,), (1,)), ((), ())),
                                  preferred_element_type=jnp.float32)


def _in_proj(x2, g, w_nat, w_tr, tm):
    s = x2.shape[0]
    return pl.pallas_call(
        _in_proj_kernel,
        grid=(s // tm,),
        in_specs=[
            pl.BlockSpec((tm, D_MODEL), lambda i: (i, 0)),
            pl.BlockSpec((1, D_MODEL), lambda i: (0, 0)),
            pl.BlockSpec((D_MODEL, N_NAT), lambda i: (0, 0)),
            pl.BlockSpec((N_TR, D_MODEL), lambda i: (0, 0)),
        ],
        out_specs=[
            pl.BlockSpec((tm, N_NAT), lambda i: (i, 0)),
            pl.BlockSpec((N_TR, tm), lambda i: (0, i)),
        ],
        out_shape=[
            jax.ShapeDtypeStruct((s, N_NAT), jnp.float32),
            jax.ShapeDtypeStruct((N_TR, s), jnp.float32),
        ],
        compiler_params=pltpu.CompilerParams(
            dimension_semantics=("arbitrary",), vmem_limit_bytes=VMEM_LIMIT),
        name="in_proj",
    )(x2, g, w_nat, w_tr)


def _rms_rows(v, gain):
    ms = jnp.mean(v * v, axis=0, keepdims=True)
    return v * lax.rsqrt(ms + EPS) * gain


def _rope_rows(v, cos, sin):
    half = v.shape[0] // 2
    x1, x2 = v[:half], v[half:]
    return x1 * cos - x2 * sin, x2 * cos + x1 * sin


def _prep_kernel(pt_ref, pos_ref, inv_a_ref, inv_d_ref, qng_ref, wuq_ref, kvng_ref, wukv_ref,
                 qg_ref, kng_ref, krg_ref, dqg_ref, dkg_ref,
                 q_ref, k_ref, v_ref, kmean_ref, *, tm):
    i = pl.program_id(0)
    nb_tile = tm // BLK

    @pl.when(i == 0)
    def _():
        kmean_ref[...] = jnp.zeros_like(kmean_ref)

    pos = pos_ref[...].astype(jnp.float32)
    ang_a = inv_a_ref[...] * pos
    cos_a, sin_a = jnp.cos(ang_a), jnp.sin(ang_a)
    ang_d = inv_d_ref[...] * pos
    cos_d, sin_d = jnp.cos(ang_d), jnp.sin(ang_d)

    zeros32 = jnp.zeros((DK - MLA_QK, tm), jnp.float32)
    ones_rows = jnp.ones((DVX - DV, BLK), jnp.bfloat16)

    cqn = _rms_rows(pt_ref[TR_CQ:TR_CQ + MLA_Q_RANK, :], qng_ref[...]).astype(jnp.bfloat16)
    q_all = jnp.dot(wuq_ref[...], cqn, preferred_element_type=jnp.float32)
    ckvn = _rms_rows(pt_ref[TR_CKV:TR_CKV + MLA_KV_RANK, :], kvng_ref[...]).astype(jnp.bfloat16)
    kv_all = jnp.dot(wukv_ref[...], ckvn, preferred_element_type=jnp.float32)
    krn = _rms_rows(pt_ref[TR_KR:TR_KR + MLA_ROPE, :], krg_ref[...])
    kr1, kr2 = _rope_rows(krn, cos_a, sin_a)
    scale_a = MLA_QK ** -0.5 * LOG2E
    for h in range(MLA_HEADS):
        qn = _rms_rows(q_all[h * MLA_QK:(h + 1) * MLA_QK], qg_ref[...])
        r1, r2 = _rope_rows(qn[MLA_NOPE:], cos_a, sin_a)
        q_ext = jnp.concatenate([qn[:MLA_NOPE], r1, r2, zeros32], axis=0) * scale_a
        q_ref[h] = q_ext.astype(jnp.bfloat16)
        base = h * (MLA_NOPE + MLA_V)
        kn = _rms_rows(kv_all[base:base + MLA_NOPE], kng_ref[...])
        k_ext = jnp.concatenate([kn, kr1, kr2, zeros32], axis=0)
        k_nat = k_ext.T.astype(jnp.bfloat16)
        vv = kv_all[base + MLA_NOPE:base + MLA_NOPE + MLA_V].astype(jnp.bfloat16)
        for b in range(nb_tile):
            k_ref[h, b] = k_nat[b * BLK:(b + 1) * BLK]
            v_ref[h, b, 0:DV, :] = vv[:, b * BLK:(b + 1) * BLK]
            v_ref[h, b, DV:DVX, :] = ones_rows

    nblk = kmean_ref.shape[1]
    j_iota = lax.broadcasted_iota(jnp.int32, (nblk, tm), 0)
    t_iota = lax.broadcasted_iota(jnp.int32, (1, tm), 1)
    qblk = i * nb_tile + t_iota // BLK
    onehot = (j_iota == qblk).astype(jnp.float32)
    past = j_iota < qblk
    row_iota = lax.broadcasted_iota(jnp.int32, (nblk, DK), 0)
    zeros64 = jnp.zeros((DK - MOBA_HD, tm), jnp.float32)
    scale_d = MOBA_HD ** -0.5 * LOG2E
    for h in range(MOBA_HEADS):
        hh = MLA_HEADS + h
        kn = _rms_rows(pt_ref[TR_MK + h * MOBA_HD:TR_MK + (h + 1) * MOBA_HD, :], dkg_ref[...])
        k1, k2 = _rope_rows(kn[:MOBA_ROT], cos_d, sin_d)
        k_ext = jnp.concatenate([k1, k2, kn[MOBA_ROT:], onehot], axis=0)
        k_nat = k_ext.T
        kmean = kmean_ref[h]
        for b in range(nb_tile):
            kb = k_nat[b * BLK:(b + 1) * BLK]
            k_ref[hh, b] = kb.astype(jnp.bfloat16)
            row = jnp.mean(kb, axis=0, keepdims=True)
            kmean = jnp.where(row_iota == i * nb_tile + b, row, kmean)
        kmean_ref[h] = kmean
        vv = pt_ref[TR_MV + h * MOBA_HD:TR_MV + (h + 1) * MOBA_HD, :].astype(jnp.bfloat16)
        for b in range(nb_tile):
            v_ref[hh, b, 0:DV, :] = vv[:, b * BLK:(b + 1) * BLK]
            v_ref[hh, b, DV:DVX, :] = ones_rows

        qn = _rms_rows(pt_ref[TR_MQ + h * MOBA_HD:TR_MQ + (h + 1) * MOBA_HD, :], dqg_ref[...])
        q1, q2 = _rope_rows(qn[:MOBA_ROT], cos_d, sin_d)
        qd = jnp.concatenate([q1, q2, qn[MOBA_ROT:]], axis=0) * scale_d
        gate = jnp.dot(kmean, jnp.concatenate([qd, zeros64], axis=0),
                       preferred_element_type=jnp.float32, precision=lax.Precision.HIGHEST)
        gm = jnp.where(past, gate, NEG)
        sel = j_iota == qblk
        for r in range(MOBA_TOPK):
            m = jnp.max(gm, axis=0, keepdims=True)
            cand = jnp.where(gm == m, j_iota, nblk)
            jmin = jnp.min(cand, axis=0, keepdims=True)
            pick = j_iota == jmin
            sel = sel | (pick & (qblk > r))
            gm = jnp.where(pick, -jnp.inf, gm)
        bias = jnp.where(sel, 0.0, NEG)
        q_ref[hh] = jnp.concatenate([qd, bias], axis=0).astype(jnp.bfloat16)


def _prep(pt, pos, inv_a, inv_d, qng, wuq_t, kvng, wukv_t, qg, kng, krg, dqg, dkg, tm):
    s = pt.shape[1]
    nblk = s // BLK
    assert nblk <= DK - MOBA_HD
    nb_tile = tm // BLK
    full = lambda a: pl.BlockSpec(a.shape, lambda i: (0,) * a.ndim)
    small = (inv_a, inv_d, qng, wuq_t, kvng, wukv_t, qg, kng, krg, dqg, dkg)
    return pl.pallas_call(
        functools.partial(_prep_kernel, tm=tm),
        grid=(s // tm,),
        in_specs=[pl.BlockSpec((N_TR, tm), lambda i: (0, i)),
                  pl.BlockSpec((1, tm), lambda i: (0, i))] + [full(a) for a in small],
        out_specs=[
            pl.BlockSpec((N_HEADS, DK, tm), lambda i: (0, 0, i)),
            pl.BlockSpec((N_HEADS, nb_tile, BLK, DK), lambda i: (0, i, 0, 0)),
            pl.BlockSpec((N_HEADS, nb_tile, DVX, BLK), lambda i: (0, i, 0, 0)),
        ],
        out_shape=[
            jax.ShapeDtypeStruct((N_HEADS, DK, s), jnp.bfloat16),
            jax.ShapeDtypeStruct((N_HEADS, nblk, BLK, DK), jnp.bfloat16),
            jax.ShapeDtypeStruct((N_HEADS, nblk, DVX, BLK), jnp.bfloat16),
        ],
        scratch_shapes=[pltpu.VMEM((MOBA_HEADS, DK - MOBA_HD, DK), jnp.float32)],
        compiler_params=pltpu.CompilerParams(
            dimension_semantics=("arbitrary",), vmem_limit_bytes=VMEM_LIMIT),
        name="prep",
    )(pt, pos, *small)


ATTN_HG = 4
ATTN_TQ = BLK
N_DIAG = ATTN_TQ // BLK
ATTN_UNROLL_LONG = 16
ATTN_UNROLL = 4
STEP_ORDER = (("sm", 0), ("qk", 0), ("sm", 1), ("qk", 1), ("qk", 2), ("pv", 0),
              ("sm", 2), ("qk", 3), ("pv", 1), ("sm", 3), ("pv", 2), ("pv", 3))


def _attn_kernel(q_ref, k_ref, v_ref, o_ref, s_ref, smax_ref, acc_ref, m_ref):
    i = pl.program_id(1)
    n_full = i * N_DIAG

    def scores(h, j, slot):
        s = jnp.dot(k_ref[h, j], q_ref[h], preferred_element_type=jnp.float32)
        s_ref[slot, h] = s
        smax_ref[slot, h] = jnp.max(s, axis=0, keepdims=True)

    def softmax(h, slot, diag):
        m = m_ref[h]
        s = s_ref[slot, h]
        if diag is not None:
            key_i = lax.broadcasted_iota(jnp.int32, (BLK, ATTN_TQ), 0) + diag * BLK
            qry_i = lax.broadcasted_iota(jnp.int32, (BLK, ATTN_TQ), 1)
            s = jnp.where(key_i <= qry_i, s, NEG)
            m_new = jnp.maximum(m, jnp.max(s, axis=0, keepdims=True))
        else:
            m_new = jnp.maximum(m, smax_ref[slot, h])
        m_ref[h] = m_new
        return jnp.exp2(m - m_new), jnp.exp2(s - m_new).astype(jnp.bfloat16)

    def pv(h, j, alpha, p):
        acc_ref[h] = alpha * acc_ref[h] + jnp.dot(v_ref[h, j], p,
                                                  preferred_element_type=jnp.float32)

    def step(j, slot, diag=None, last=False):
        probs = {}
        for kind, h in STEP_ORDER:
            if kind == "sm":
                probs[h] = softmax(h, slot, diag)
            elif kind == "pv":
                pv(h, j, *probs[h])
            elif not last:
                scores(h, j + 1, 1 - slot)

    m_ref[...] = jnp.full_like(m_ref, -jnp.inf)
    acc_ref[...] = jnp.zeros_like(acc_ref)
    for h in range(ATTN_HG):
        scores(h, 0, 0)

    base = 0
    for unroll in (ATTN_UNROLL_LONG, ATTN_UNROLL):
        def group(t, carry, unroll=unroll, base=base):
            for u in range(unroll):
                step(base + unroll * t + u, u % 2)
            return carry

        n_groups = (n_full - base) // unroll
        lax.fori_loop(0, n_groups, group, 0)
        base = base + n_groups * unroll

    for r in range(0, ATTN_UNROLL, N_DIAG):
        @pl.when(n_full - base == r)
        def _():
            for u in range(r):
                step(base + u, u % 2)
            for d in range(N_DIAG):
                step(n_full + d, (r + d) % 2, diag=d, last=d == N_DIAG - 1)
            for h in range(ATTN_HG):
                o_ref[h] = (acc_ref[h, 0:DV, :] / acc_ref[h, DV:DV + 1, :]).astype(o_ref.dtype)


def _attn(q, k, v):
    nh, _, s = q.shape
    nblk = s // BLK
    hg, tq = ATTN_HG, ATTN_TQ
    assert ATTN_UNROLL % 2 == 0 and ATTN_UNROLL % N_DIAG == 0 and s % tq == 0
    resident = dict(pipeline_mode=pl.Buffered(1))
    return pl.pallas_call(
        _attn_kernel,
        grid=(nh // hg, s // tq),
        in_specs=[
            pl.BlockSpec((hg, DK, tq), lambda g, i: (g, 0, i)),
            pl.BlockSpec((hg, nblk, BLK, DK), lambda g, i: (g, 0, 0, 0), **resident),
            pl.BlockSpec((hg, nblk, DVX, BLK), lambda g, i: (g, 0, 0, 0), **resident),
        ],
        out_specs=pl.BlockSpec((hg, DV, tq), lambda g, i: (g, 0, i)),
        out_shape=jax.ShapeDtypeStruct((nh, DV, s), jnp.bfloat16),
        scratch_shapes=[pltpu.VMEM((2, hg, BLK, tq), jnp.float32),
                        pltpu.VMEM((2, hg, 1, tq), jnp.float32),
                        pltpu.VMEM((hg, DVX, tq), jnp.float32),
                        pltpu.VMEM((hg, 1, tq), jnp.float32)],
        compiler_params=pltpu.CompilerParams(
            dimension_semantics=("arbitrary", "arbitrary"), vmem_limit_bytes=VMEM_LIMIT),
        name="attn",
    )(q, k, v)


def _silu(g):
    return g * (1.0 / (1.0 + jnp.exp(-g)))


def _out_proj_kernel(pn_ref, o_ref, x_ref, cw_ref, lng_ref, lnb_ref, sgw_ref, sgb_ref, wo_ref,
                     y_ref, zbuf_ref, *, tm):
    i = pl.program_id(0)
    HALO = 8

    def col(k):
        return pn_ref[:, k * GW:(k + 1) * GW]

    g_a, c_h, c_b, c_c, g_b, sg_u, sg_v, g_c, g_d = (col(k) for k in range(9))

    @pl.when(i == 0)
    def _():
        zbuf_ref[0:HALO, :] = jnp.zeros((HALO, GW), jnp.float32)

    z = c_c * c_h
    zbuf_ref[HALO:HALO + tm, :] = z
    z1 = zbuf_ref[HALO - 1:HALO - 1 + tm, :]
    z2 = zbuf_ref[HALO - 2:HALO - 2 + tm, :]
    cw = cw_ref[...]
    o_b = c_b * (cw[0:1] * z2 + cw[1:2] * z1 + cw[2:3] * z)
    zbuf_ref[0:HALO, :] = z[tm - HALO:tm]

    mu = jnp.mean(sg_v, axis=-1, keepdims=True)
    dv = sg_v - mu
    var = jnp.mean(dv * dv, axis=-1, keepdims=True)
    vn = (dv * lax.rsqrt(var + EPS) * lng_ref[...] + lnb_ref[...]).astype(jnp.bfloat16)
    t_i = lax.broadcasted_iota(jnp.int32, (SG_CHUNK, SG_CHUNK), 0)
    s_i = lax.broadcasted_iota(jnp.int32, (SG_CHUNK, SG_CHUNK), 1)
    lane_grp = lax.broadcasted_iota(jnp.int32, (SG_CHUNK, GW), 1) // SG_GDIM
    ws = [jnp.where(s_i <= t_i, sgw_ref[g], 0.0).astype(jnp.bfloat16) for g in range(SG_GROUPS)]
    mixed = []
    for c in range(tm // SG_CHUNK):
        vc = vn[c * SG_CHUNK:(c + 1) * SG_CHUNK]
        mc = jnp.dot(ws[0], vc, preferred_element_type=jnp.float32)
        for g in range(1, SG_GROUPS):
            mg = jnp.dot(ws[g], vc, preferred_element_type=jnp.float32)
            mc = jnp.where(lane_grp == g, mg, mc)
        mixed.append(mc + sgb_ref[...])
    o_c = sg_u * jnp.concatenate(mixed, axis=0)

    o_a = jnp.concatenate([o_ref[h].astype(jnp.float32) for h in range(MLA_HEADS)], axis=0).T
    o_d = jnp.concatenate([o_ref[MLA_HEADS + h].astype(jnp.float32)
                           for h in range(MOBA_HEADS)], axis=0).T

    y = jnp.concatenate([o_a * _silu(g_a), o_b * _silu(g_b), o_c * _silu(g_c), o_d * _silu(g_d)],
                        axis=-1).astype(jnp.bfloat16)
    y_ref[...] = x_ref[...] + jnp.dot(y, wo_ref[...], preferred_element_type=jnp.float32)


def _out_proj(pn, o, x2, conv_w, lng, lnb, sg_w, sgb_full, w_out, tm):
    s = x2.shape[0]
    full = lambda a: pl.BlockSpec(a.shape, lambda i: (0,) * a.ndim)
    small = (conv_w, lng, lnb, sg_w, sgb_full, w_out)
    return pl.pallas_call(
        functools.partial(_out_proj_kernel, tm=tm),
        grid=(s // tm,),
        in_specs=[pl.BlockSpec((tm, N_NAT), lambda i: (i, 0)),
                  pl.BlockSpec((N_HEADS, DV, tm), lambda i: (0, 0, i)),
                  pl.BlockSpec((tm, D_MODEL), lambda i: (i, 0))] + [full(a) for a in small],
        out_specs=pl.BlockSpec((tm, D_MODEL), lambda i: (i, 0)),
        out_shape=jax.ShapeDtypeStruct((s, D_MODEL), jnp.float32),
        scratch_shapes=[pltpu.VMEM((tm + 8, GW), jnp.float32)],
        compiler_params=pltpu.CompilerParams(
            dimension_semantics=("arbitrary",), vmem_limit_bytes=VMEM_LIMIT),
        name="out_proj",
    )(pn, o, x2, *small)


def _rope_inv(rd):
    half = rd // 2
    inv = jnp.power(ROPE_THETA, -jnp.arange(half, dtype=jnp.float32) * 2.0 / rd)
    return inv.reshape(half, 1)


def _layer(x2, pos, norm_g, w_in, mla_q_norm_g, mla_w_uq, mla_kv_norm_g, mla_w_ukv, mla_q_g,
           mla_k_nope_g, mla_k_rope_g, conv_w, sg_ln_g, sg_ln_b, sg_w, sg_b, moba_q_g, moba_k_g,
           w_out, tm):
    cr = _col_ranges()
    bf = jnp.bfloat16
    w_nat = jnp.concatenate([w_in[:, cr[n][0]:cr[n][1]] for n in _NAT], axis=1).astype(bf)
    w_tr = jnp.concatenate([w_in[:, cr[n][0]:cr[n][1]] for n in _TR], axis=1).T.astype(bf)
    col = lambda a: a.reshape(-1, 1)
    pn, pt = _in_proj(x2, norm_g.reshape(1, -1), w_nat, w_tr, tm)
    q, k, v = _prep(pt, pos, _rope_inv(MLA_ROPE), _rope_inv(MOBA_ROT),
                    col(mla_q_norm_g), mla_w_uq.T.astype(bf), col(mla_kv_norm_g),
                    mla_w_ukv.T.astype(bf), col(mla_q_g), col(mla_k_nope_g), col(mla_k_rope_g),
                    col(moba_q_g), col(moba_k_g), 2 * tm)
    o = _attn(q, k, v)
    sgb_full = jnp.repeat(sg_b.T, SG_GDIM, axis=1)
    return _out_proj(pn, o, x2, conv_w, sg_ln_g.reshape(1, -1), sg_ln_b.reshape(1, -1), sg_w,
                     sgb_full, w_out.astype(bf), 2 * tm)


def kernel(x, positions, norm_g, w_in, mla_q_norm_g, mla_w_uq, mla_kv_norm_g, mla_w_ukv, mla_q_g, mla_k_nope_g, mla_k_rope_g, conv_w, sg_ln_g, sg_ln_b, sg_w, sg_b, moba_q_g, moba_k_g, w_out):
    b, s, d = x.shape
    assert b == 1 and d == D_MODEL and s % 512 == 0
    tm = 512
    x2 = x.reshape(s, d)
    pos = positions.reshape(1, s)
    params = (norm_g, w_in, mla_q_norm_g, mla_w_uq, mla_kv_norm_g, mla_w_ukv, mla_q_g,
              mla_k_nope_g, mla_k_rope_g, conv_w, sg_ln_g, sg_ln_b, sg_w, sg_b, moba_q_g,
              moba_k_g, w_out)
    for layer in range(norm_g.shape[0]):
        x2 = _layer(x2, pos, *(p[layer] for p in params), tm)
    return x2.reshape(b, s, d)
```

```python
import functools

import jax
import jax.numpy as jnp
from jax import lax
from jax.experimental import pallas as pl
from jax.experimental.pallas import tpu as pltpu

D_MODEL = 1024
GW = 256
EPS = 1e-6
NEG = -1e30
ROPE_THETA = 500000.0

MLA_HEADS = 4
MLA_NOPE = 64
MLA_ROPE = 32
MLA_QK = MLA_NOPE + MLA_ROPE
MLA_V = 64
MLA_Q_RANK = 256
MLA_KV_RANK = 128

CONV_WIDTH = 3
SG_CHUNK = 128
SG_GROUPS = 4
SG_GDIM = 64

MOBA_HEADS = 4
MOBA_HD = 64
MOBA_ROT = 16
MOBA_BLOCK = 256
MOBA_TOPK = 3

N_HEADS = MLA_HEADS + MOBA_HEADS
DK = 128
DV = 64
DVX = DV + 16
LOG2E = 1.4426950408889634
BLK = MOBA_BLOCK

_SPLITS = (256, 128, 32, 256, 256, 256, 256, 256, 256, 256, 256, 256, 256, 256, 256)
_NAMES = ("cq", "ckv", "kr", "g_a", "c_h", "c_b", "c_c", "g_b", "sg_u", "sg_v", "g_c",
          "mq", "mk", "mv", "g_d")
_NAT = ("g_a", "c_h", "c_b", "c_c", "g_b", "sg_u", "sg_v", "g_c", "g_d")
_TR = ("cq", "ckv", "kr", "mq", "mk", "mv")
N_NAT = 9 * GW
N_TR = 256 + 128 + 32 + 3 * 256
TR_CQ, TR_CKV, TR_KR, TR_MQ, TR_MK, TR_MV = 0, 256, 384, 416, 672, 928

VMEM_LIMIT = 56 * 1024 * 1024


def _col_ranges():
    out, o = {}, 0
    for n, s in zip(_NAMES, _SPLITS):
        out[n] = (o, o + s)
        o += s
    return out


def _in_proj_kernel(x_ref, g_ref, wn_ref, wt_ref, pn_ref, pt_ref):
    x = x_ref[...]
    ms = jnp.mean(x * x, axis=-1, keepdims=True)
    h = (x * lax.rsqrt(ms + EPS) * g_ref[...]).astype(jnp.bfloat16)
    pn_ref[...] = jnp.dot(h, wn_ref[...], preferred_element_type=jnp.float32)
    pt_ref[...] = lax.dot_general(wt_ref[...], h, (((1,), (1,)), ((), ())),
                                  preferred_element_type=jnp.float32)


def _in_proj(x2, g, w_nat, w_tr, tm):
    s = x2.shape[0]
    return pl.pallas_call(
        _in_proj_kernel,
        grid=(s // tm,),
        in_specs=[
            pl.BlockSpec((tm, D_MODEL), lambda i: (i, 0)),
            pl.BlockSpec((1, D_MODEL), lambda i: (0, 0)),
            pl.BlockSpec((D_MODEL, N_NAT), lambda i: (0, 0)),
            pl.BlockSpec((N_TR, D_MODEL), lambda i: (0, 0)),
        ],
        out_specs=[
            pl.BlockSpec((tm, N_NAT), lambda i: (i, 0)),
            pl.BlockSpec((N_TR, tm), lambda i: (0, i)),
        ],
        out_shape=[
            jax.ShapeDtypeStruct((s, N_NAT), jnp.float32),
            jax.ShapeDtypeStruct((N_TR, s), jnp.float32),
        ],
        compiler_params=pltpu.CompilerParams(
            dimension_semantics=("arbitrary",), vmem_limit_bytes=VMEM_LIMIT),
        name="in_proj",
    )(x2, g, w_nat, w_tr)


def _rms_rows(v, gain):
    ms = jnp.mean(v * v, axis=0, keepdims=True)
    return v * lax.rsqrt(ms + EPS) * gain


def _rope_rows(v, cos, sin):
    half = v.shape[0] // 2
    x1, x2 = v[:half], v[half:]
    return x1 * cos - x2 * sin, x2 * cos + x1 * sin


def _prep_kernel(pt_ref, pos_ref, inv_a_ref, inv_d_ref, qng_ref, wuq_ref, kvng_ref, wukv_ref,
                 qg_ref, kng_ref, krg_ref, dqg_ref, dkg_ref,
                 q_ref, k_ref, v_ref, kmean_ref, *, tm):
    i = pl.program_id(0)
    nb_tile = tm // BLK

    @pl.when(i == 0)
    def _():
        kmean_ref[...] = jnp.zeros_like(kmean_ref)

    pos = pos_ref[...].astype(jnp.float32)
    ang_a = inv_a_ref[...] * pos
    cos_a, sin_a = jnp.cos(ang_a), jnp.sin(ang_a)
    ang_d = inv_d_ref[...] * pos
    cos_d, sin_d = jnp.cos(ang_d), jnp.sin(ang_d)

    zeros32 = jnp.zeros((DK - MLA_QK, tm), jnp.float32)
    ones_rows = jnp.ones((DVX - DV, BLK), jnp.bfloat16)

    cqn = _rms_rows(pt_ref[TR_CQ:TR_CQ + MLA_Q_RANK, :], qng_ref[...]).astype(jnp.bfloat16)
    q_all = jnp.dot(wuq_ref[...], cqn, preferred_element_type=jnp.float32)
    ckvn = _rms_rows(pt_ref[TR_CKV:TR_CKV + MLA_KV_RANK, :], kvng_ref[...]).astype(jnp.bfloat16)
    kv_all = jnp.dot(wukv_ref[...], ckvn, preferred_element_type=jnp.float32)
    krn = _rms_rows(pt_ref[TR_KR:TR_KR + MLA_ROPE, :], krg_ref[...])
    kr1, kr2 = _rope_rows(krn, cos_a, sin_a)
    scale_a = MLA_QK ** -0.5 * LOG2E
    for h in range(MLA_HEADS):
        qn = _rms_rows(q_all[h * MLA_QK:(h + 1) * MLA_QK], qg_ref[...])
        r1, r2 = _rope_rows(qn[MLA_NOPE:], cos_a, sin_a)
        q_ext = jnp.concatenate([qn[:MLA_NOPE], r1, r2, zeros32], axis=0) * scale_a
        q_ref[h] = q_ext.astype(jnp.bfloat16)
        base = h * (MLA_NOPE + MLA_V)
        kn = _rms_rows(kv_all[base:base + MLA_NOPE], kng_ref[...])
        k_ext = jnp.concatenate([kn, kr1, kr2, zeros32], axis=0)
        k_nat = k_ext.T.astype(jnp.bfloat16)
        vv = kv_all[base + MLA_NOPE:base + MLA_NOPE + MLA_V].astype(jnp.bfloat16)
        for b in range(nb_tile):
            k_ref[h, b] = k_nat[b * BLK:(b + 1) * BLK]
            v_ref[h, b, 0:DV, :] = vv[:, b * BLK:(b + 1) * BLK]
            v_ref[h, b, DV:DVX, :] = ones_rows

    nblk = kmean_ref.shape[1]
    j_iota = lax.broadcasted_iota(jnp.int32, (nblk, tm), 0)
    t_iota = lax.broadcasted_iota(jnp.int32, (1, tm), 1)
    qblk = i * nb_tile + t_iota // BLK
    onehot = (j_iota == qblk).astype(jnp.float32)
    past = j_iota < qblk
    row_iota = lax.broadcasted_iota(jnp.int32, (nblk, DK), 0)
    zeros64 = jnp.zeros((DK - MOBA_HD, tm), jnp.float32)
    scale_d = MOBA_HD ** -0.5 * LOG2E
    for h in range(MOBA_HEADS):
        hh = MLA_HEADS + h
        kn = _rms_rows(pt_ref[TR_MK + h * MOBA_HD:TR_MK + (h + 1) * MOBA_HD, :], dkg_ref[...])
        k1, k2 = _rope_rows(kn[:MOBA_ROT], cos_d, sin_d)
        k_ext = jnp.concatenate([k1, k2, kn[MOBA_ROT:], onehot], axis=0)
        k_nat = k_ext.T
        kmean = kmean_ref[h]
        for b in range(nb_tile):
            kb = k_nat[b * BLK:(b + 1) * BLK]
            k_ref[hh, b] = kb.astype(jnp.bfloat16)
            row = jnp.mean(kb, axis=0, keepdims=True)
            kmean = jnp.where(row_iota == i * nb_tile + b, row, kmean)
        kmean_ref[h] = kmean
        vv = pt_ref[TR_MV + h * MOBA_HD:TR_MV + (h + 1) * MOBA_HD, :].astype(jnp.bfloat16)
        for b in range(nb_tile):
            v_ref[hh, b, 0:DV, :] = vv[:, b * BLK:(b + 1) * BLK]
            v_ref[hh, b, DV:DVX, :] = ones_rows

        qn = _rms_rows(pt_ref[TR_MQ + h * MOBA_HD:TR_MQ + (h + 1) * MOBA_HD, :], dqg_ref[...])
        q1, q2 = _rope_rows(qn[:MOBA_ROT], cos_d, sin_d)
        qd = jnp.concatenate([q1, q2, qn[MOBA_ROT:]], axis=0) * scale_d
        gate = jnp.dot(kmean, jnp.concatenate([qd, zeros64], axis=0),
                       preferred_element_type=jnp.float32, precision=lax.Precision.HIGHEST)
        gm = jnp.where(past, gate, NEG)
        sel = j_iota == qblk
        for r in range(MOBA_TOPK):
            m = jnp.max(gm, axis=0, keepdims=True)
            cand = jnp.where(gm == m, j_iota, nblk)
            jmin = jnp.min(cand, axis=0, keepdims=True)
            pick = j_iota == jmin
            sel = sel | (pick & (qblk > r))
            gm = jnp.where(pick, -jnp.inf, gm)
        bias = jnp.where(sel, 0.0, NEG)
        q_ref[hh] = jnp.concatenate([qd, bias], axis=0).astype(jnp.bfloat16)


def _prep(pt, pos, inv_a, inv_d, qng, wuq_t, kvng, wukv_t, qg, kng, krg, dqg, dkg, tm):
    s = pt.shape[1]
    nblk = s // BLK
    assert nblk <= DK - MOBA_HD
    nb_tile = tm // BLK
    full = lambda a: pl.BlockSpec(a.shape, lambda i: (0,) * a.ndim)
    small = (inv_a, inv_d, qng, wuq_t, kvng, wukv_t, qg, kng, krg, dqg, dkg)
    return pl.pallas_call(
        functools.partial(_prep_kernel, tm=tm),
        grid=(s // tm,),
        in_specs=[pl.BlockSpec((N_TR, tm), lambda i: (0, i)),
                  pl.BlockSpec((1, tm), lambda i: (0, i))] + [full(a) for a in small],
        out_specs=[
            pl.BlockSpec((N_HEADS, DK, tm), lambda i: (0, 0, i)),
            pl.BlockSpec((N_HEADS, nb_tile, BLK, DK), lambda i: (0, i, 0, 0)),
            pl.BlockSpec((N_HEADS, nb_tile, DVX, BLK), lambda i: (0, i, 0, 0)),
        ],
        out_shape=[
            jax.ShapeDtypeStruct((N_HEADS, DK, s), jnp.bfloat16),
            jax.ShapeDtypeStruct((N_HEADS, nblk, BLK, DK), jnp.bfloat16),
            jax.ShapeDtypeStruct((N_HEADS, nblk, DVX, BLK), jnp.bfloat16),
        ],
        scratch_shapes=[pltpu.VMEM((MOBA_HEADS, DK - MOBA_HD, DK), jnp.float32)],
        compiler_params=pltpu.CompilerParams(
            dimension_semantics=("arbitrary",), vmem_limit_bytes=VMEM_LIMIT),
        name="prep",
    )(pt, pos, *small)


ATTN_HG = 4
ATTN_TQ = BLK
N_DIAG = ATTN_TQ // BLK
ATTN_LEVELS = (16, 8, 4)
ATTN_UNROLL = ATTN_LEVELS[-1]
STEP_ORDER = (("sm", 0), ("qk", 0), ("sm", 1), ("qk", 1), ("qk", 2), ("pv", 0),
              ("sm", 2), ("qk", 3), ("pv", 1), ("sm", 3), ("pv", 2), ("pv", 3))


def _attn_kernel(q_ref, k_ref, v_ref, o_ref, s_ref, smax_ref, acc_ref, m_ref):
    i = pl.program_id(1)
    n_full = i * N_DIAG

    def scores(h, j, slot):
        s = jnp.dot(k_ref[h, j], q_ref[h], preferred_element_type=jnp.float32)
        s_ref[slot, h] = s
        smax_ref[slot, h] = jnp.max(s, axis=0, keepdims=True)

    def softmax(h, slot, diag):
        m = m_ref[h]
        s = s_ref[slot, h]
        if diag is not None:
            key_i = lax.broadcasted_iota(jnp.int32, (BLK, ATTN_TQ), 0) + diag * BLK
            qry_i = lax.broadcasted_iota(jnp.int32, (BLK, ATTN_TQ), 1)
            s = jnp.where(key_i <= qry_i, s, NEG)
            m_new = jnp.maximum(m, jnp.max(s, axis=0, keepdims=True))
        else:
            m_new = jnp.maximum(m, smax_ref[slot, h])
        m_ref[h] = m_new
        return jnp.exp2(m - m_new), jnp.exp2(s - m_new).astype(jnp.bfloat16)

    def pv(h, j, alpha, p):
        acc_ref[h] = alpha * acc_ref[h] + jnp.dot(v_ref[h, j], p,
                                                  preferred_element_type=jnp.float32)

    def step(j, slot, diag=None, last=False):
        probs = {}
        for kind, h in STEP_ORDER:
            if kind == "sm":
                probs[h] = softmax(h, slot, diag)
            elif kind == "pv":
                pv(h, j, *probs[h])
            elif not last:
                scores(h, j + 1, 1 - slot)

    m_ref[...] = jnp.full_like(m_ref, -jnp.inf)
    acc_ref[...] = jnp.zeros_like(acc_ref)
    for h in range(ATTN_HG):
        scores(h, 0, 0)

    base = 0
    for unroll in ATTN_LEVELS:
        def group(t, carry, unroll=unroll, base=base):
            for u in range(unroll):
                step(base + unroll * t + u, u % 2)
            return carry

        n_groups = (n_full - base) // unroll
        lax.fori_loop(0, n_groups, group, 0)
        base = base + n_groups * unroll

    for r in range(0, ATTN_UNROLL, N_DIAG):
        @pl.when(n_full - base == r)
        def _():
            for u in range(r):
                step(base + u, u % 2)
            for d in range(N_DIAG):
                step(n_full + d, (r + d) % 2, diag=d, last=d == N_DIAG - 1)
            for h in range(ATTN_HG):
                o_ref[h] = (acc_ref[h, 0:DV, :] / acc_ref[h, DV:DV + 1, :]).astype(o_ref.dtype)


def _attn(q, k, v):
    nh, _, s = q.shape
    nblk = s // BLK
    hg, tq = ATTN_HG, ATTN_TQ
    assert all(u % 2 == 0 for u in ATTN_LEVELS) and ATTN_UNROLL % N_DIAG == 0 and s % tq == 0
    resident = dict(pipeline_mode=pl.Buffered(1))
    return pl.pallas_call(
        _attn_kernel,
        grid=(nh // hg, s // tq),
        in_specs=[
            pl.BlockSpec((hg, DK, tq), lambda g, i: (g, 0, i)),
            pl.BlockSpec((hg, nblk, BLK, DK), lambda g, i: (g, 0, 0, 0), **resident),
            pl.BlockSpec((hg, nblk, DVX, BLK), lambda g, i: (g, 0, 0, 0), **resident),
        ],
        out_specs=pl.BlockSpec((hg, DV, tq), lambda g, i: (g, 0, i)),
        out_shape=jax.ShapeDtypeStruct((nh, DV, s), jnp.bfloat16),
        scratch_shapes=[pltpu.VMEM((2, hg, BLK, tq), jnp.float32),
                        pltpu.VMEM((2, hg, 1, tq), jnp.float32),
                        pltpu.VMEM((hg, DVX, tq), jnp.float32),
                        pltpu.VMEM((hg, 1, tq), jnp.float32)],
        compiler_params=pltpu.CompilerParams(
            dimension_semantics=("arbitrary", "arbitrary"), vmem_limit_bytes=VMEM_LIMIT),
        name="attn",
    )(q, k, v)


def _silu(g):
    return g * (1.0 / (1.0 + jnp.exp(-g)))


def _out_proj_kernel(pn_ref, o_ref, x_ref, cw_ref, lng_ref, lnb_ref, sgw_ref, sgb_ref, wo_ref,
                     y_ref, zbuf_ref, *, tm):
    i = pl.program_id(0)
    HALO = 8

    def col(k):
        return pn_ref[:, k * GW:(k + 1) * GW]

    g_a, c_h, c_b, c_c, g_b, sg_u, sg_v, g_c, g_d = (col(k) for k in range(9))

    @pl.when(i == 0)
    def _():
        zbuf_ref[0:HALO, :] = jnp.zeros((HALO, GW), jnp.float32)

    z = c_c * c_h
    zbuf_ref[HALO:HALO + tm, :] = z
    z1 = zbuf_ref[HALO - 1:HALO - 1 + tm, :]
    z2 = zbuf_ref[HALO - 2:HALO - 2 + tm, :]
    cw = cw_ref[...]
    o_b = c_b * (cw[0:1] * z2 + cw[1:2] * z1 + cw[2:3] * z)
    zbuf_ref[0:HALO, :] = z[tm - HALO:tm]

    mu = jnp.mean(sg_v, axis=-1, keepdims=True)
    dv = sg_v - mu
    var = jnp.mean(dv * dv, axis=-1, keepdims=True)
    vn = (dv * lax.rsqrt(var + EPS) * lng_ref[...] + lnb_ref[...]).astype(jnp.bfloat16)
    t_i = lax.broadcasted_iota(jnp.int32, (SG_CHUNK, SG_CHUNK), 0)
    s_i = lax.broadcasted_iota(jnp.int32, (SG_CHUNK, SG_CHUNK), 1)
    lane_grp = lax.broadcasted_iota(jnp.int32, (SG_CHUNK, GW), 1) // SG_GDIM
    ws = [jnp.where(s_i <= t_i, sgw_ref[g], 0.0).astype(jnp.bfloat16) for g in range(SG_GROUPS)]
    mixed = []
    for c in range(tm // SG_CHUNK):
        vc = vn[c * SG_CHUNK:(c + 1) * SG_CHUNK]
        mc = jnp.dot(ws[0], vc, preferred_element_type=jnp.float32)
        for g in range(1, SG_GROUPS):
            mg = jnp.dot(ws[g], vc, preferred_element_type=jnp.float32)
            mc = jnp.where(lane_grp == g, mg, mc)
        mixed.append(mc + sgb_ref[...])
    o_c = sg_u * jnp.concatenate(mixed, axis=0)

    o_a = jnp.concatenate([o_ref[h].astype(jnp.float32) for h in range(MLA_HEADS)], axis=0).T
    o_d = jnp.concatenate([o_ref[MLA_HEADS + h].astype(jnp.float32)
                           for h in range(MOBA_HEADS)], axis=0).T

    y = jnp.concatenate([o_a * _silu(g_a), o_b * _silu(g_b), o_c * _silu(g_c), o_d * _silu(g_d)],
                        axis=-1).astype(jnp.bfloat16)
    y_ref[...] = x_ref[...] + jnp.dot(y, wo_ref[...], preferred_element_type=jnp.float32)


def _out_proj(pn, o, x2, conv_w, lng, lnb, sg_w, sgb_full, w_out, tm):
    s = x2.shape[0]
    full = lambda a: pl.BlockSpec(a.shape, lambda i: (0,) * a.ndim)
    small = (conv_w, lng, lnb, sg_w, sgb_full, w_out)
    return pl.pallas_call(
        functools.partial(_out_proj_kernel, tm=tm),
        grid=(s // tm,),
        in_specs=[pl.BlockSpec((tm, N_NAT), lambda i: (i, 0)),
                  pl.BlockSpec((N_HEADS, DV, tm), lambda i: (0, 0, i)),
                  pl.BlockSpec((tm, D_MODEL), lambda i: (i, 0))] + [full(a) for a in small],
        out_specs=pl.BlockSpec((tm, D_MODEL), lambda i: (i, 0)),
        out_shape=jax.ShapeDtypeStruct((s, D_MODEL), jnp.float32),
        scratch_shapes=[pltpu.VMEM((tm + 8, GW), jnp.float32)],
        compiler_params=pltpu.CompilerParams(
            dimension_semantics=("arbitrary",), vmem_limit_bytes=VMEM_LIMIT),
        name="out_proj",
    )(pn, o, x2, *small)


def _rope_inv(rd):
    half = rd // 2
    inv = jnp.power(ROPE_THETA, -jnp.arange(half, dtype=jnp.float32) * 2.0 / rd)
    return inv.reshape(half, 1)


def _layer(x2, pos, norm_g, w_in, mla_q_norm_g, mla_w_uq, mla_kv_norm_g, mla_w_ukv, mla_q_g,
           mla_k_nope_g, mla_k_rope_g, conv_w, sg_ln_g, sg_ln_b, sg_w, sg_b, moba_q_g, moba_k_g,
           w_out, tm):
    cr = _col_ranges()
    bf = jnp.bfloat16
    w_nat = jnp.concatenate([w_in[:, cr[n][0]:cr[n][1]] for n in _NAT], axis=1).astype(bf)
    w_tr = jnp.concatenate([w_in[:, cr[n][0]:cr[n][1]] for n in _TR], axis=1).T.astype(bf)
    col = lambda a: a.reshape(-1, 1)
    pn, pt = _in_proj(x2, norm_g.reshape(1, -1), w_nat, w_tr, tm)
    q, k, v = _prep(pt, pos, _rope_inv(MLA_ROPE), _rope_inv(MOBA_ROT),
                    col(mla_q_norm_g), mla_w_uq.T.astype(bf), col(mla_kv_norm_g),
                    mla_w_ukv.T.astype(bf), col(mla_q_g), col(mla_k_nope_g), col(mla_k_rope_g),
                    col(moba_q_g), col(moba_k_g), 2 * tm)
    o = _attn(q, k, v)
    sgb_full = jnp.repeat(sg_b.T, SG_GDIM, axis=1)
    return _out_proj(pn, o, x2, conv_w, sg_ln_g.reshape(1, -1), sg_ln_b.reshape(1, -1), sg_w,
                     sgb_full, w_out.astype(bf), 2 * tm)


def kernel(x, positions, norm_g, w_in, mla_q_norm_g, mla_w_uq, mla_kv_norm_g, mla_w_ukv, mla_q_g, mla_k_nope_g, mla_k_rope_g, conv_w, sg_ln_g, sg_ln_b, sg_w, sg_b, moba_q_g, moba_k_g, w_out):
    b, s, d = x.shape
    assert b == 1 and d == D_MODEL and s % 512 == 0
    tm = 512
    x2 = x.reshape(s, d)
    pos = positions.reshape(1, s)
    params = (norm_g, w_in, mla_q_norm_g, mla_w_uq, mla_kv_norm_g, mla_w_ukv, mla_q_g,
              mla_k_nope_g, mla_k_rope_g, conv_w, sg_ln_g, sg_ln_b, sg_w, sg_b, moba_q_g,
              moba_k_g, w_out)
    for layer in range(norm_g.shape[0]):
        x2 = _layer(x2, pos, *(p[layer] for p in params), tm)
    return x2.reshape(b, s, d)
```

```python
import functools

import jax
import jax.numpy as jnp
from jax import lax
from jax.experimental import pallas as pl
from jax.experimental.pallas import tpu as pltpu

D_MODEL = 1024
GW = 256
EPS = 1e-6
NEG = -1e30
ROPE_THETA = 500000.0

MLA_HEADS = 4
MLA_NOPE = 64
MLA_ROPE = 32
MLA_QK = MLA_NOPE + MLA_ROPE
MLA_V = 64
MLA_Q_RANK = 256
MLA_KV_RANK = 128

CONV_WIDTH = 3
SG_CHUNK = 128
SG_GROUPS = 4
SG_GDIM = 64

MOBA_HEADS = 4
MOBA_HD = 64
MOBA_ROT = 16
MOBA_BLOCK = 256
MOBA_TOPK = 3

N_HEADS = MLA_HEADS + MOBA_HEADS
DK = 128
DV = 64
DVX = DV + 16
LOG2E = 1.4426950408889634
BLK = MOBA_BLOCK

_SPLITS = (256, 128, 32, 256, 256, 256, 256, 256, 256, 256, 256, 256, 256, 256, 256)
_NAMES = ("cq", "ckv", "kr", "g_a", "c_h", "c_b", "c_c", "g_b", "sg_u", "sg_v", "g_c",
          "mq", "mk", "mv", "g_d")
_NAT = ("g_a", "c_h", "c_b", "c_c", "g_b", "sg_u", "sg_v", "g_c", "g_d")
_TR = ("cq", "ckv", "kr", "mq", "mk", "mv")
N_NAT = 9 * GW
N_TR = 256 + 128 + 32 + 3 * 256
TR_CQ, TR_CKV, TR_KR, TR_MQ, TR_MK, TR_MV = 0, 256, 384, 416, 672, 928

VMEM_LIMIT = 56 * 1024 * 1024


def _col_ranges():
    out, o = {}, 0
    for n, s in zip(_NAMES, _SPLITS):
        out[n] = (o, o + s)
        o += s
    return out


def _in_proj_kernel(x_ref, g_ref, wn_ref, wt_ref, pn_ref, pt_ref):
    x = x_ref[...]
    ms = jnp.mean(x * x, axis=-1, keepdims=True)
    h = (x * lax.rsqrt(ms + EPS) * g_ref[...]).astype(jnp.bfloat16)
    pn_ref[...] = jnp.dot(h, wn_ref[...], preferred_element_type=jnp.float32)
    pt_ref[...] = lax.dot_general(wt_ref[...], h, (((1,), (1,)), ((), ())),
                                  preferred_element_type=jnp.float32)


def _in_proj(x2, g, w_nat, w_tr, tm):
    s = x2.shape[0]
    return pl.pallas_call(
        _in_proj_kernel,
        grid=(s // tm,),
        in_specs=[
            pl.BlockSpec((tm, D_MODEL), lambda i: (i, 0)),
            pl.BlockSpec((1, D_MODEL), lambda i: (0, 0)),
            pl.BlockSpec((D_MODEL, N_NAT), lambda i: (0, 0)),
            pl.BlockSpec((N_TR, D_MODEL), lambda i: (0, 0)),
        ],
        out_specs=[
            pl.BlockSpec((tm, N_NAT), lambda i: (i, 0)),
            pl.BlockSpec((N_TR, tm), lambda i: (0, i)),
        ],
        out_shape=[
            jax.ShapeDtypeStruct((s, N_NAT), jnp.float32),
            jax.ShapeDtypeStruct((N_TR, s), jnp.float32),
        ],
        compiler_params=pltpu.CompilerParams(
            dimension_semantics=("arbitrary",), vmem_limit_bytes=VMEM_LIMIT),
        name="in_proj",
    )(x2, g, w_nat, w_tr)


def _rms_rows(v, gain):
    ms = jnp.mean(v * v, axis=0, keepdims=True)
    return v * lax.rsqrt(ms + EPS) * gain


def _rope_rows(v, cos, sin):
    half = v.shape[0] // 2
    x1, x2 = v[:half], v[half:]
    return x1 * cos - x2 * sin, x2 * cos + x1 * sin


def _prep_kernel(pt_ref, pos_ref, inv_a_ref, inv_d_ref, qng_ref, wuq_ref, kvng_ref, wukv_ref,
                 qg_ref, kng_ref, krg_ref, dqg_ref, dkg_ref,
                 q_ref, k_ref, v_ref, kmean_ref, *, tm):
    i = pl.program_id(0)
    nb_tile = tm // BLK

    @pl.when(i == 0)
    def _():
        kmean_ref[...] = jnp.zeros_like(kmean_ref)

    pos = pos_ref[...].astype(jnp.float32)
    ang_a = inv_a_ref[...] * pos
    cos_a, sin_a = jnp.cos(ang_a), jnp.sin(ang_a)
    ang_d = inv_d_ref[...] * pos
    cos_d, sin_d = jnp.cos(ang_d), jnp.sin(ang_d)

    zeros32 = jnp.zeros((DK - MLA_QK, tm), jnp.float32)
    ones_rows = jnp.ones((DVX - DV, BLK), jnp.bfloat16)

    cqn = _rms_rows(pt_ref[TR_CQ:TR_CQ + MLA_Q_RANK, :], qng_ref[...]).astype(jnp.bfloat16)
    q_all = jnp.dot(wuq_ref[...], cqn, preferred_element_type=jnp.float32)
    ckvn = _rms_rows(pt_ref[TR_CKV:TR_CKV + MLA_KV_RANK, :], kvng_ref[...]).astype(jnp.bfloat16)
    kv_all = jnp.dot(wukv_ref[...], ckvn, preferred_element_type=jnp.float32)
    krn = _rms_rows(pt_ref[TR_KR:TR_KR + MLA_ROPE, :], krg_ref[...])
    kr1, kr2 = _rope_rows(krn, cos_a, sin_a)
    scale_a = MLA_QK ** -0.5 * LOG2E
    for h in range(MLA_HEADS):
        qn = _rms_rows(q_all[h * MLA_QK:(h + 1) * MLA_QK], qg_ref[...])
        r1, r2 = _rope_rows(qn[MLA_NOPE:], cos_a, sin_a)
        q_ext = jnp.concatenate([qn[:MLA_NOPE], r1, r2, zeros32], axis=0) * scale_a
        q_ref[h] = q_ext.astype(jnp.bfloat16)
        base = h * (MLA_NOPE + MLA_V)
        kn = _rms_rows(kv_all[base:base + MLA_NOPE], kng_ref[...])
        k_ext = jnp.concatenate([kn, kr1, kr2, zeros32], axis=0)
        k_nat = k_ext.T.astype(jnp.bfloat16)
        vv = kv_all[base + MLA_NOPE:base + MLA_NOPE + MLA_V].astype(jnp.bfloat16)
        for b in range(nb_tile):
            k_ref[h, b] = k_nat[b * BLK:(b + 1) * BLK]
            v_ref[h, b, 0:DV, :] = vv[:, b * BLK:(b + 1) * BLK]
            v_ref[h, b, DV:DVX, :] = ones_rows

    nblk = kmean_ref.shape[1]
    j_iota = lax.broadcasted_iota(jnp.int32, (nblk, tm), 0)
    t_iota = lax.broadcasted_iota(jnp.int32, (1, tm), 1)
    qblk = i * nb_tile + t_iota // BLK
    onehot = (j_iota == qblk).astype(jnp.float32)
    past = j_iota < qblk
    row_iota = lax.broadcasted_iota(jnp.int32, (nblk, DK), 0)
    scale_d = MOBA_HD ** -0.5 * LOG2E
    for h in range(MOBA_HEADS):
        hh = MLA_HEADS + h
        kn = _rms_rows(pt_ref[TR_MK + h * MOBA_HD:TR_MK + (h + 1) * MOBA_HD, :], dkg_ref[...])
        k1, k2 = _rope_rows(kn[:MOBA_ROT], cos_d, sin_d)
        k_ext = jnp.concatenate([k1, k2, kn[MOBA_ROT:], onehot], axis=0)
        k_nat = k_ext.T
        kmean = kmean_ref[h]
        for b in range(nb_tile):
            kb = k_nat[b * BLK:(b + 1) * BLK]
            k_ref[hh, b] = kb.astype(jnp.bfloat16)
            row = jnp.mean(kb, axis=0, keepdims=True)
            kmean = jnp.where(row_iota == i * nb_tile + b, row, kmean)
        kmean_ref[h] = kmean
        vv = pt_ref[TR_MV + h * MOBA_HD:TR_MV + (h + 1) * MOBA_HD, :].astype(jnp.bfloat16)
        for b in range(nb_tile):
            v_ref[hh, b, 0:DV, :] = vv[:, b * BLK:(b + 1) * BLK]
            v_ref[hh, b, DV:DVX, :] = ones_rows

        qn = _rms_rows(pt_ref[TR_MQ + h * MOBA_HD:TR_MQ + (h + 1) * MOBA_HD, :], dqg_ref[...])
        q1, q2 = _rope_rows(qn[:MOBA_ROT], cos_d, sin_d)
        qd = jnp.concatenate([q1, q2, qn[MOBA_ROT:]], axis=0) * scale_d
        gate = jnp.dot(kmean[:, 0:MOBA_HD], qd,
                       preferred_element_type=jnp.float32, precision=lax.Precision.HIGHEST)
        gm = jnp.where(past, gate, NEG)
        sel = j_iota == qblk
        for r in range(MOBA_TOPK):
            m = jnp.max(gm, axis=0, keepdims=True)
            cand = jnp.where(gm == m, j_iota, nblk)
            jmin = jnp.min(cand, axis=0, keepdims=True)
            pick = j_iota == jmin
            sel = sel | (pick & (qblk > r))
            gm = jnp.where(pick, -jnp.inf, gm)
        bias = jnp.where(sel, 0.0, NEG)
        q_ref[hh] = jnp.concatenate([qd, bias], axis=0).astype(jnp.bfloat16)


def _prep(pt, pos, inv_a, inv_d, qng, wuq_t, kvng, wukv_t, qg, kng, krg, dqg, dkg, tm):
    s = pt.shape[1]
    nblk = s // BLK
    assert nblk <= DK - MOBA_HD
    nb_tile = tm // BLK
    full = lambda a: pl.BlockSpec(a.shape, lambda i: (0,) * a.ndim)
    small = (inv_a, inv_d, qng, wuq_t, kvng, wukv_t, qg, kng, krg, dqg, dkg)
    return pl.pallas_call(
        functools.partial(_prep_kernel, tm=tm),
        grid=(s // tm,),
        in_specs=[pl.BlockSpec((N_TR, tm), lambda i: (0, i)),
                  pl.BlockSpec((1, tm), lambda i: (0, i))] + [full(a) for a in small],
        out_specs=[
            pl.BlockSpec((N_HEADS, DK, tm), lambda i: (0, 0, i)),
            pl.BlockSpec((N_HEADS, nb_tile, BLK, DK), lambda i: (0, i, 0, 0)),
            pl.BlockSpec((N_HEADS, nb_tile, DVX, BLK), lambda i: (0, i, 0, 0)),
        ],
        out_shape=[
            jax.ShapeDtypeStruct((N_HEADS, DK, s), jnp.bfloat16),
            jax.ShapeDtypeStruct((N_HEADS, nblk, BLK, DK), jnp.bfloat16),
            jax.ShapeDtypeStruct((N_HEADS, nblk, DVX, BLK), jnp.bfloat16),
        ],
        scratch_shapes=[pltpu.VMEM((MOBA_HEADS, DK - MOBA_HD, DK), jnp.float32)],
        compiler_params=pltpu.CompilerParams(
            dimension_semantics=("arbitrary",), vmem_limit_bytes=VMEM_LIMIT),
        name="prep",
    )(pt, pos, *small)


ATTN_HG = 4
ATTN_TQ = BLK
N_DIAG = ATTN_TQ // BLK
ATTN_LEVELS = (16, 8, 4)
ATTN_UNROLL = ATTN_LEVELS[-1]
STEP_ORDER = (("sm", 0), ("qk", 0), ("sm", 1), ("qk", 1), ("qk", 2), ("pv", 0),
              ("sm", 2), ("qk", 3), ("pv", 1), ("sm", 3), ("pv", 2), ("pv", 3))


KV_CHUNK = 8


def _attn_kernel(q_ref, k_hbm, v_hbm, o_ref, k_ref, v_ref, kv_sem, s_ref, smax_ref, acc_ref, m_ref):
    i = pl.program_id(1)
    n_full = i * N_DIAG

    g = pl.program_id(0)
    n_chunks = k_ref.shape[1] // KV_CHUNK

    def kv_copies(c):
        blocks = pl.ds(c * KV_CHUNK, KV_CHUNK)
        heads = pl.ds(g * ATTN_HG, ATTN_HG)
        return (pltpu.make_async_copy(k_hbm.at[heads, blocks], k_ref.at[:, blocks], kv_sem.at[0, c]),
                pltpu.make_async_copy(v_hbm.at[heads, blocks], v_ref.at[:, blocks], kv_sem.at[1, c]))

    @pl.when(i == 0)
    def _():
        for c in range(n_chunks):
            for cp in kv_copies(c):
                cp.start()

    @pl.when(i % KV_CHUNK == 0)
    def _():
        for cp in kv_copies(i // KV_CHUNK):
            cp.wait()

    def scores(h, j, slot):
        s = jnp.dot(k_ref[h, j], q_ref[h], preferred_element_type=jnp.float32)
        s_ref[slot, h] = s
        smax_ref[slot, h] = jnp.max(s, axis=0, keepdims=True)

    def softmax(h, slot, diag):
        m = m_ref[h]
        s = s_ref[slot, h]
        if diag is not None:
            key_i = lax.broadcasted_iota(jnp.int32, (BLK, ATTN_TQ), 0) + diag * BLK
            qry_i = lax.broadcasted_iota(jnp.int32, (BLK, ATTN_TQ), 1)
            s = jnp.where(key_i <= qry_i, s, NEG)
            m_new = jnp.maximum(m, jnp.max(s, axis=0, keepdims=True))
        else:
            m_new = jnp.maximum(m, smax_ref[slot, h])
        m_ref[h] = m_new
        return jnp.exp2(m - m_new), jnp.exp2(s - m_new).astype(jnp.bfloat16)

    def pv(h, j, alpha, p):
        acc_ref[h] = alpha * acc_ref[h] + jnp.dot(v_ref[h, j], p,
                                                  preferred_element_type=jnp.float32)

    def step(j, slot, diag=None, last=False):
        probs = {}
        for kind, h in STEP_ORDER:
            if kind == "sm":
                probs[h] = softmax(h, slot, diag)
            elif kind == "pv":
                pv(h, j, *probs[h])
            elif not last:
                scores(h, j + 1, 1 - slot)

    m_ref[...] = jnp.full_like(m_ref, -jnp.inf)
    acc_ref[...] = jnp.zeros_like(acc_ref)
    for h in range(ATTN_HG):
        scores(h, 0, 0)

    base = 0
    for unroll in ATTN_LEVELS:
        def group(t, carry, unroll=unroll, base=base):
            for u in range(unroll):
                step(base + unroll * t + u, u % 2)
            return carry

        n_groups = (n_full - base) // unroll
        lax.fori_loop(0, n_groups, group, 0)
        base = base + n_groups * unroll

    for r in range(0, ATTN_UNROLL, N_DIAG):
        @pl.when(n_full - base == r)
        def _():
            for u in range(r):
                step(base + u, u % 2)
            for d in range(N_DIAG):
                step(n_full + d, (r + d) % 2, diag=d, last=d == N_DIAG - 1)
            for h in range(ATTN_HG):
                o_ref[h] = (acc_ref[h, 0:DV, :] / acc_ref[h, DV:DV + 1, :]).astype(o_ref.dtype)


def _attn(q, k, v):
    nh, _, s = q.shape
    nblk = s // BLK
    hg, tq = ATTN_HG, ATTN_TQ
    assert all(u % 2 == 0 for u in ATTN_LEVELS) and ATTN_UNROLL % N_DIAG == 0 and s % tq == 0
    assert all(sorted(h for kind, h in STEP_ORDER if kind == k) == list(range(hg))
               for k in ("sm", "qk", "pv"))
    assert nblk % KV_CHUNK == 0 and N_DIAG == 1
    return pl.pallas_call(
        _attn_kernel,
        grid=(nh // hg, s // tq),
        in_specs=[
            pl.BlockSpec((hg, DK, tq), lambda g, i: (g, 0, i)),
            pl.BlockSpec(memory_space=pl.ANY),
            pl.BlockSpec(memory_space=pl.ANY),
        ],
        out_specs=pl.BlockSpec((hg, DV, tq), lambda g, i: (g, 0, i)),
        out_shape=jax.ShapeDtypeStruct((nh, DV, s), jnp.bfloat16),
        scratch_shapes=[pltpu.VMEM((hg, nblk, BLK, DK), jnp.bfloat16),
                        pltpu.VMEM((hg, nblk, DVX, BLK), jnp.bfloat16),
                        pltpu.SemaphoreType.DMA((2, nblk // KV_CHUNK)),
                        pltpu.VMEM((2, hg, BLK, tq), jnp.float32),
                        pltpu.VMEM((2, hg, 1, tq), jnp.float32),
                        pltpu.VMEM((hg, DVX, tq), jnp.float32),
                        pltpu.VMEM((hg, 1, tq), jnp.float32)],
        compiler_params=pltpu.CompilerParams(
            dimension_semantics=("arbitrary", "arbitrary"), vmem_limit_bytes=VMEM_LIMIT),
        name="attn",
    )(q, k, v)


def _silu(g):
    return g * (1.0 / (1.0 + jnp.exp(-g)))


def _out_proj_kernel(pn_ref, o_ref, x_ref, cw_ref, lng_ref, lnb_ref, sgw_ref, sgb_ref, wo_ref,
                     y_ref, zbuf_ref, *, tm):
    i = pl.program_id(0)
    HALO = 8

    def col(k):
        return pn_ref[:, k * GW:(k + 1) * GW]

    g_a, c_h, c_b, c_c, g_b, sg_u, sg_v, g_c, g_d = (col(k) for k in range(9))

    @pl.when(i == 0)
    def _():
        zbuf_ref[0:HALO, :] = jnp.zeros((HALO, GW), jnp.float32)

    z = c_c * c_h
    zbuf_ref[HALO:HALO + tm, :] = z
    z1 = zbuf_ref[HALO - 1:HALO - 1 + tm, :]
    z2 = zbuf_ref[HALO - 2:HALO - 2 + tm, :]
    cw = cw_ref[...]
    o_b = c_b * (cw[0:1] * z2 + cw[1:2] * z1 + cw[2:3] * z)
    zbuf_ref[0:HALO, :] = z[tm - HALO:tm]

    mu = jnp.mean(sg_v, axis=-1, keepdims=True)
    dv = sg_v - mu
    var = jnp.mean(dv * dv, axis=-1, keepdims=True)
    vn = (dv * lax.rsqrt(var + EPS) * lng_ref[...] + lnb_ref[...]).astype(jnp.bfloat16)
    t_i = lax.broadcasted_iota(jnp.int32, (SG_CHUNK, SG_CHUNK), 0)
    s_i = lax.broadcasted_iota(jnp.int32, (SG_CHUNK, SG_CHUNK), 1)
    lane_grp = lax.broadcasted_iota(jnp.int32, (SG_CHUNK, GW), 1) // SG_GDIM
    ws = [jnp.where(s_i <= t_i, sgw_ref[g], 0.0).astype(jnp.bfloat16) for g in range(SG_GROUPS)]
    mixed = []
    for c in range(tm // SG_CHUNK):
        vc = vn[c * SG_CHUNK:(c + 1) * SG_CHUNK]
        mc = jnp.dot(ws[0], vc, preferred_element_type=jnp.float32)
        for g in range(1, SG_GROUPS):
            mg = jnp.dot(ws[g], vc, preferred_element_type=jnp.float32)
            mc = jnp.where(lane_grp == g, mg, mc)
        mixed.append(mc + sgb_ref[...])
    o_c = sg_u * jnp.concatenate(mixed, axis=0)

    o_a = jnp.concatenate([o_ref[h].astype(jnp.float32) for h in range(MLA_HEADS)], axis=0).T
    o_d = jnp.concatenate([o_ref[MLA_HEADS + h].astype(jnp.float32)
                           for h in range(MOBA_HEADS)], axis=0).T

    y = jnp.concatenate([o_a * _silu(g_a), o_b * _silu(g_b), o_c * _silu(g_c), o_d * _silu(g_d)],
                        axis=-1).astype(jnp.bfloat16)
    y_ref[...] = x_ref[...] + jnp.dot(y, wo_ref[...], preferred_element_type=jnp.float32)


def _out_proj(pn, o, x2, conv_w, lng, lnb, sg_w, sgb_full, w_out, tm):
    s = x2.shape[0]
    full = lambda a: pl.BlockSpec(a.shape, lambda i: (0,) * a.ndim)
    small = (conv_w, lng, lnb, sg_w, sgb_full, w_out)
    return pl.pallas_call(
        functools.partial(_out_proj_kernel, tm=tm),
        grid=(s // tm,),
        in_specs=[pl.BlockSpec((tm, N_NAT), lambda i: (i, 0)),
                  pl.BlockSpec((N_HEADS, DV, tm), lambda i: (0, 0, i)),
                  pl.BlockSpec((tm, D_MODEL), lambda i: (i, 0))] + [full(a) for a in small],
        out_specs=pl.BlockSpec((tm, D_MODEL), lambda i: (i, 0)),
        out_shape=jax.ShapeDtypeStruct((s, D_MODEL), jnp.float32),
        scratch_shapes=[pltpu.VMEM((tm + 8, GW), jnp.float32)],
        compiler_params=pltpu.CompilerParams(
            dimension_semantics=("arbitrary",), vmem_limit_bytes=VMEM_LIMIT),
        name="out_proj",
    )(pn, o, x2, *small)


def _rope_inv(rd):
    half = rd // 2
    inv = jnp.power(ROPE_THETA, -jnp.arange(half, dtype=jnp.float32) * 2.0 / rd)
    return inv.reshape(half, 1)


def _layer(x2, pos, norm_g, w_in, mla_q_norm_g, mla_w_uq, mla_kv_norm_g, mla_w_ukv, mla_q_g,
           mla_k_nope_g, mla_k_rope_g, conv_w, sg_ln_g, sg_ln_b, sg_w, sg_b, moba_q_g, moba_k_g,
           w_out, tm):
    cr = _col_ranges()
    bf = jnp.bfloat16
    w_nat = jnp.concatenate([w_in[:, cr[n][0]:cr[n][1]] for n in _NAT], axis=1).astype(bf)
    w_tr = jnp.concatenate([w_in[:, cr[n][0]:cr[n][1]] for n in _TR], axis=1).T.astype(bf)
    col = lambda a: a.reshape(-1, 1)
    pn, pt = _in_proj(x2, norm_g.reshape(1, -1), w_nat, w_tr, tm)
    q, k, v = _prep(pt, pos, _rope_inv(MLA_ROPE), _rope_inv(MOBA_ROT),
                    col(mla_q_norm_g), mla_w_uq.T.astype(bf), col(mla_kv_norm_g),
                    mla_w_ukv.T.astype(bf), col(mla_q_g), col(mla_k_nope_g), col(mla_k_rope_g),
                    col(moba_q_g), col(moba_k_g), 2 * tm)
    o = _attn(q, k, v)
    sgb_full = jnp.repeat(sg_b.T, SG_GDIM, axis=1)
    return _out_proj(pn, o, x2, conv_w, sg_ln_g.reshape(1, -1), sg_ln_b.reshape(1, -1), sg_w,
                     sgb_full, w_out.astype(bf), 2 * tm)


def kernel(x, positions, norm_g, w_in, mla_q_norm_g, mla_w_uq, mla_kv_norm_g, mla_w_ukv, mla_q_g, mla_k_nope_g, mla_k_rope_g, conv_w, sg_ln_g, sg_ln_b, sg_w, sg_b, moba_q_g, moba_k_g, w_out):
    b, s, d = x.shape
    assert b == 1 and d == D_MODEL and s % 512 == 0
    tm = 512
    x2 = x.reshape(s, d)
    pos = positions.reshape(1, s)
    params = (norm_g, w_in, mla_q_norm_g, mla_w_uq, mla_kv_norm_g, mla_w_ukv, mla_q_g,
              mla_k_nope_g, mla_k_rope_g, conv_w, sg_ln_g, sg_ln_b, sg_w, sg_b, moba_q_g,
              moba_k_g, w_out)
    for layer in range(norm_g.shape[0]):
        x2 = _layer(x2, pos, *(p[layer] for p in params), tm)
    return x2.reshape(b, s, d)
```
